```python
import jax, jax.numpy as jnp
from jax import lax
import numpy as np

D_MODEL = 2048
BATCH = 16
SEQ = 2048
DEPTH = 2
DEC_BATCH = 8
DEC_SEQ = 32
PAST_LEN = 4096

CHUNK = 64
N_HEADS = 16
HEAD_DIM = 128
SB_WIDTH = N_HEADS * HEAD_DIM
SB_SCALE = HEAD_DIM ** -0.5
Q_BLOCK = 128
LRU_WIDTH = D_MODEL
LRU_BLOCKS = 16
LRU_BLOCK_W = LRU_WIDTH // LRU_BLOCKS
CONV_W = 4
LRU_C = 8.0
N_GROUPS = 4
EXPERTS_PER_GROUP = 4
N_EXPERTS = N_GROUPS * EXPERTS_PER_GROUP
TOP_K = 2
EXPERT_FF = 512
NORM_EPS = 1e-6
IN_COLS = 3 * SB_WIDTH + 2 * LRU_WIDTH + 2 * D_MODEL

kernel_name = 'sb_rglru_hmoe_stream_step'


def rmsnorm(x, g):
    xf = x.astype(jnp.float32)
    y = xf * lax.rsqrt(jnp.mean(xf * xf, axis=-1, keepdims=True) + NORM_EPS)
    return (y * g.astype(jnp.float32)).astype(x.dtype)


def sb_block(q, k, v, q_start):
    tq, lk = q.shape[1], k.shape[1]
    z = jnp.einsum('bqhd,bkhd->bhqk', q.astype(jnp.float32), k.astype(jnp.float32)) * SB_SCALE
    q_pos = q_start + jnp.arange(tq)[:, None]
    k_pos = jnp.arange(lk)[None, :]
    mask = k_pos < q_pos
    log_keep = jnp.where(mask, jax.nn.log_sigmoid(-z), 0.0)
    log_rest = lax.cumsum(log_keep, axis=3, reverse=True) - log_keep
    weights = jnp.where(mask, jnp.exp(jax.nn.log_sigmoid(z) + log_rest), 0.0)
    return jnp.einsum('bhqk,bkhd->bqhd', weights.astype(v.dtype), v)


def sb_attention(q, k, v, q_offset):
    tq = q.shape[1]
    outs = []
    for b0 in range(0, tq, Q_BLOCK):
        b1 = min(b0 + Q_BLOCK, tq)
        kend = q_offset + b1
        outs.append(sb_block(q[:, b0:b1], k[:, :kend], v[:, :kend], q_offset + b0))
    return jnp.concatenate(outs, axis=1)


def causal_conv(xb, prev, w, b):
    t = xb.shape[1]
    xpad = jnp.concatenate([prev.astype(xb.dtype), xb], axis=1)
    out = b
    for tap in range(CONV_W):
        out = out + xpad[:, tap:tap + t] * w[tap]
    return out, xpad[:, -(CONV_W - 1):]


def _lin_combine(left, right):
    a_l, b_l = left
    a_r, b_r = right
    return a_l * a_r, a_r * b_l + b_r


def rg_lru(xc, h0, wa, ba, wx, bx, lam):
    bn, t, _ = xc.shape
    xf = xc.astype(jnp.float32)
    xblk = xf.reshape(bn, t, LRU_BLOCKS, LRU_BLOCK_W)
    r = jax.nn.sigmoid(jnp.einsum('btnc,ncd->btnd', xblk, wa.astype(jnp.float32)).reshape(bn, t, LRU_WIDTH) + ba.astype(jnp.float32))
    i = jax.nn.sigmoid(jnp.einsum('btnc,ncd->btnd', xblk, wx.astype(jnp.float32)).reshape(bn, t, LRU_WIDTH) + bx.astype(jnp.float32))
    log_a = LRU_C * r * jax.nn.log_sigmoid(lam.astype(jnp.float32))
    a = jnp.exp(log_a)
    u = jnp.sqrt(-jnp.expm1(2.0 * log_a)) * (i * xf)
    u = u.at[:, 0].add(a[:, 0] * h0.astype(jnp.float32))
    _, h = lax.associative_scan(_lin_combine, (a, u), axis=1)
    return h.astype(xc.dtype), h[:, -1].astype(xc.dtype)


def hmoe(h, wg, bg, ws, bs, wu, wd):
    bn, t, d = h.shape
    hf = h.reshape(-1, d)
    g_logits = (hf @ wg).astype(jnp.float32) + bg.astype(jnp.float32)
    g_prob = jax.nn.softmax(g_logits, axis=-1)
    _, g_idx = lax.top_k(g_logits, 1)
    p_g = jnp.take_along_axis(g_prob, g_idx, axis=1)
    sub = jnp.einsum('nd,gde->nge', hf, ws).astype(jnp.float32) + bs.astype(jnp.float32)
    sub_sel = jnp.einsum('nge,ng->ne', sub, jax.nn.one_hot(g_idx[:, 0], N_GROUPS, dtype=jnp.float32))
    e_val, e_idx = lax.top_k(sub_sel, TOP_K)
    e_w = jax.nn.softmax(e_val, axis=-1) * p_g
    e_id = g_idx * EXPERTS_PER_GROUP + e_idx
    combine = jnp.sum(jax.nn.one_hot(e_id, N_EXPERTS, dtype=jnp.float32) * e_w[..., None], axis=1).astype(h.dtype)
    out = jnp.zeros_like(hf)
    for e in range(N_EXPERTS):
        gate, up = jnp.split(hf @ wu[e], 2, axis=-1)
        out = out + combine[:, e:e + 1] * ((jax.nn.silu(gate) * up) @ wd[e])
    return out.reshape(bn, t, d)


def run_trunk(x, past_k, past_v, conv_prev, h_prev, params):
    (norm1_g, w_in, conv_w, conv_b, lru_wa, lru_ba, lru_wx, lru_bx, lru_lambda,
     w_pa, w_pb, w_o, norm2_g, router_group_w, router_group_b, router_sub_w,
     router_sub_b, expert_w_up, expert_w_down, final_norm_g) = params
    bn, t, _ = x.shape
    q_offset = 0 if past_k is None else past_k.shape[2]
    splits = [SB_WIDTH, 2 * SB_WIDTH, 3 * SB_WIDTH, 3 * SB_WIDTH + LRU_WIDTH,
              3 * SB_WIDTH + 2 * LRU_WIDTH, 3 * SB_WIDTH + 2 * LRU_WIDTH + D_MODEL]
    ks, vs, convs, hs = [], [], [], []
    for l in range(DEPTH):
        h = rmsnorm(x, norm1_g[l])
        proj = h @ w_in[l]
        q, k, v, xb, yg, ga, gb = jnp.split(proj, splits, axis=-1)
        q = q.reshape(bn, t, N_HEADS, HEAD_DIM)
        k = k.reshape(bn, t, N_HEADS, HEAD_DIM)
        v = v.reshape(bn, t, N_HEADS, HEAD_DIM)
        if past_k is None:
            k_all, v_all = k, v
        else:
            k_all = jnp.concatenate([past_k[l].astype(k.dtype), k], axis=1)
            v_all = jnp.concatenate([past_v[l].astype(v.dtype), v], axis=1)
        o_a = sb_attention(q, k_all, v_all, q_offset).reshape(bn, t, SB_WIDTH)
        xc, conv_new = causal_conv(xb, conv_prev[l], conv_w[l], conv_b[l])
        h_seq, h_last = rg_lru(xc, h_prev[l], lru_wa[l], lru_ba[l], lru_wx[l], lru_bx[l], lru_lambda[l])
        o_b = jax.nn.gelu(yg) * h_seq
        mixed = jax.nn.sigmoid(ga) * (o_a @ w_pa[l]) + jax.nn.sigmoid(gb) * (o_b @ w_pb[l])
        x = x + mixed @ w_o[l]
        x = x + hmoe(rmsnorm(x, norm2_g[l]), router_group_w[l], router_group_b[l], router_sub_w[l],
                     router_sub_b[l], expert_w_up[l], expert_w_down[l])
        ks.append(k)
        vs.append(v)
        convs.append(conv_new)
        hs.append(h_last)
    return rmsnorm(x, final_norm_g), jnp.stack(ks), jnp.stack(vs), jnp.stack(convs), jnp.stack(hs)


def setup_inputs(seed: int = 0) -> dict:
    key = jax.random.key(seed)
    ks = jax.random.split(key, 26)
    f32 = jnp.float32

    def nrm(k, shape, scale):
        return jax.random.normal(k, shape, f32) * scale

    u = jax.random.uniform(ks[13], (DEPTH, LRU_WIDTH), f32, minval=0.9, maxval=0.999)
    s = u ** (1.0 / LRU_C)
    lam = jnp.log(s) - jnp.log1p(-s)
    return {
        'x_prompt': nrm(ks[0], (BATCH, SEQ, D_MODEL), 1.0),
        'x_sample': nrm(ks[1], (DEC_BATCH, DEC_SEQ, D_MODEL), 1.0),
        'cache_k': nrm(ks[2], (DEPTH, DEC_BATCH, PAST_LEN, N_HEADS, HEAD_DIM), 1.0),
        'cache_v': nrm(ks[3], (DEPTH, DEC_BATCH, PAST_LEN, N_HEADS, HEAD_DIM), 1.0),
        'state_conv': nrm(ks[4], (DEPTH, DEC_BATCH, CONV_W - 1, LRU_WIDTH), 1.0),
        'state_lru': nrm(ks[5], (DEPTH, DEC_BATCH, LRU_WIDTH), 0.5),
        'norm1_g': 1.0 + nrm(ks[6], (DEPTH, D_MODEL), 0.02),
        'w_in': nrm(ks[7], (DEPTH, D_MODEL, IN_COLS), D_MODEL ** -0.5),
        'conv_w': nrm(ks[8], (DEPTH, CONV_W, LRU_WIDTH), CONV_W ** -0.5),
        'conv_b': nrm(ks[9], (DEPTH, LRU_WIDTH), 0.02),
        'lru_wa': nrm(ks[10], (DEPTH, LRU_BLOCKS, LRU_BLOCK_W, LRU_BLOCK_W), LRU_BLOCK_W ** -0.5),
        'lru_ba': nrm(ks[11], (DEPTH, LRU_WIDTH), 0.02),
        'lru_wx': nrm(ks[12], (DEPTH, LRU_BLOCKS, LRU_BLOCK_W, LRU_BLOCK_W), LRU_BLOCK_W ** -0.5),
        'lru_bx': nrm(ks[14], (DEPTH, LRU_WIDTH), 0.02),
        'lru_lambda': lam,
        'w_pa': nrm(ks[15], (DEPTH, SB_WIDTH, D_MODEL), SB_WIDTH ** -0.5),
        'w_pb': nrm(ks[16], (DEPTH, LRU_WIDTH, D_MODEL), LRU_WIDTH ** -0.5),
        'w_o': nrm(ks[17], (DEPTH, D_MODEL, D_MODEL), D_MODEL ** -0.5),
        'norm2_g': 1.0 + nrm(ks[18], (DEPTH, D_MODEL), 0.02),
        'router_group_w': nrm(ks[19], (DEPTH, D_MODEL, N_GROUPS), D_MODEL ** -0.5),
        'router_group_b': nrm(ks[20], (DEPTH, N_GROUPS), 0.01),
        'router_sub_w': nrm(ks[21], (DEPTH, N_GROUPS, D_MODEL, EXPERTS_PER_GROUP), D_MODEL ** -0.5),
        'router_sub_b': nrm(ks[22], (DEPTH, N_GROUPS, EXPERTS_PER_GROUP), 0.01),
        'expert_w_up': nrm(ks[23], (DEPTH, N_EXPERTS, D_MODEL, 2 * EXPERT_FF), D_MODEL ** -0.5),
        'expert_w_down': nrm(ks[24], (DEPTH, N_EXPERTS, EXPERT_FF, D_MODEL), EXPERT_FF ** -0.5),
        'final_norm_g': 1.0 + nrm(ks[25], (D_MODEL,), 0.02),
    }


def reference(x_prompt, x_sample, cache_k, cache_v, state_conv, state_lru, norm1_g, w_in,
              conv_w, conv_b, lru_wa, lru_ba, lru_wx, lru_bx, lru_lambda, w_pa, w_pb, w_o,
              norm2_g, router_group_w, router_group_b, router_sub_w, router_sub_b,
              expert_w_up, expert_w_down, final_norm_g):
    params = (norm1_g, w_in, conv_w, conv_b, lru_wa, lru_ba, lru_wx, lru_bx, lru_lambda,
              w_pa, w_pb, w_o, norm2_g, router_group_w, router_group_b, router_sub_w,
              router_sub_b, expert_w_up, expert_w_down, final_norm_g)
    bp = x_prompt.shape[0]
    zero_conv = jnp.zeros((DEPTH, bp, CONV_W - 1, LRU_WIDTH), x_prompt.dtype)
    zero_lru = jnp.zeros((DEPTH, bp, LRU_WIDTH), x_prompt.dtype)
    y_prompt, prompt_k, prompt_v, prompt_conv, prompt_lru = run_trunk(
        x_prompt, None, None, zero_conv, zero_lru, params)
    y_sample, sample_k, sample_v, sample_conv, sample_lru = run_trunk(
        x_sample, cache_k, cache_v, state_conv, state_lru, params)
    return (y_prompt, y_sample, prompt_k, prompt_v, prompt_conv, prompt_lru,
            sample_k, sample_v, sample_conv, sample_lru)
```

```python
import functools

import jax
import jax.numpy as jnp
from jax import lax
from jax.experimental import pallas as pl
from jax.experimental.pallas import tpu as pltpu

N_HEADS = 16
HEAD_DIM = 128
SB_SCALE = HEAD_DIM ** -0.5
LRU_BLOCKS = 16
CONV_W = 4
LRU_C = 8.0
N_GROUPS = 4
EXPERTS_PER_GROUP = 4
N_EXPERTS = N_GROUPS * EXPERTS_PER_GROUP
NORM_EPS = 1e-6

LANES = 128
SUBLANES = 8
VMEM_LIMIT = 56 * 1024 * 1024
SB_EXIT = -110.0
ROUTE_GROUP_LANE = 0
ROUTE_EXPERT_LANE0 = 4

f32 = jnp.float32
bf16 = jnp.bfloat16


def _pick(n, cands):
    for c in cands:
        if n % c == 0:
            return c
    raise ValueError(f"no tile size for {n}")


def _params(sem):
    return pltpu.CompilerParams(dimension_semantics=sem, vmem_limit_bytes=VMEM_LIMIT)


def _rmsnorm_val(x, g):
    ms = jnp.mean(x * x, axis=-1, keepdims=True)
    return (x * lax.rsqrt(ms + NORM_EPS)) * g


def _rmsnorm_kernel(x_ref, g_ref, o_ref):
    o_ref[...] = _rmsnorm_val(x_ref[...], g_ref[...]).astype(o_ref.dtype)


def rmsnorm(x, g, out_dtype, row0=0, nrows=None):
    n, d = x.shape
    nrows = n - row0 if nrows is None else nrows
    tm = _pick(nrows, (512, 256, 128, 64, 32, 16, 8))
    assert row0 % tm == 0
    off = row0 // tm
    return pl.pallas_call(
        _rmsnorm_kernel,
        grid=(nrows // tm,),
        in_specs=[pl.BlockSpec((tm, d), lambda m: (m + off, 0)),
                  pl.BlockSpec((1, d), lambda m: (0, 0))],
        out_specs=pl.BlockSpec((tm, d), lambda m: (m, 0)),
        out_shape=jax.ShapeDtypeStruct((nrows, d), out_dtype),
        compiler_params=_params(("parallel",)),
        name="rmsnorm",
    )(x, g.reshape(1, d))


def _mm_kernel(*refs, scale, has_res):
    if has_res:
        h_ref, w_ref, r_ref, o_ref, wb_ref = refs
    else:
        h_ref, w_ref, o_ref, wb_ref = refs

    @pl.when(pl.program_id(1) == 0)
    def _():
        wb_ref[...] = w_ref[...].astype(bf16)

    acc = jnp.dot(h_ref[...], wb_ref[...], preferred_element_type=f32)
    if scale != 1.0:
        acc = acc * scale
    if has_res:
        acc = acc + r_ref[...]
    o_ref[...] = acc.astype(o_ref.dtype)


def matmul(h, w, layer, col0, ncols, out_dtype, *, scale=1.0, res=None, row0=0, nrows=None,
           out_stack=None, out_buf=None):
    n, kdim = h.shape
    nrows = n - row0 if nrows is None else nrows
    tm = _pick(nrows, (768, 512, 256, 128, 64, 32))
    tn = _pick(ncols, (1024, 512, 256, 128))
    assert row0 % tm == 0 and col0 % tn == 0
    roff, coff = row0 // tm, col0 // tn
    in_specs = [pl.BlockSpec((tm, kdim), lambda j, m: (m + roff, 0)),
                pl.BlockSpec((None, kdim, tn), lambda j, m: (layer, 0, j + coff))]
    args = [h, w]
    if res is not None:
        assert row0 == 0 and col0 == 0
        in_specs.append(pl.BlockSpec((tm, tn), lambda j, m: (m, j)))
        args.append(res)
    aliases = {}
    if out_stack is None:
        out_spec = pl.BlockSpec((tm, tn), lambda j, m: (m, j))
        out_shape = jax.ShapeDtypeStruct((nrows, ncols), out_dtype)
    else:
        depth, idx = out_stack
        out_spec = pl.BlockSpec((None, tm, tn), lambda j, m: (idx, m, j))
        out_shape = jax.ShapeDtypeStruct((depth, nrows, ncols), out_dtype)
        if out_buf is not None:
            in_specs.append(pl.BlockSpec(memory_space=pl.ANY))
            args.append(out_buf)
            aliases = {len(args) - 1: 0}
    kern = functools.partial(_mm_kernel, scale=scale, has_res=res is not None)
    if aliases:
        kern = functools.partial(_drop_alias_ref, kern, len(args) - 1)
    return pl.pallas_call(
        kern,
        grid=(ncols // tn, nrows // tm),
        in_specs=in_specs,
        out_specs=out_spec,
        out_shape=out_shape,
        scratch_shapes=[pltpu.VMEM((kdim, tn), bf16)],
        input_output_aliases=aliases,
        compiler_params=_params(("arbitrary", "arbitrary")),
        name="matmul",
    )(*args)


def _drop_alias_ref(kern, pos, *refs):
    kern(*(refs[:pos] + refs[pos + 1:]))


def _log_sigmoid(x):
    return jnp.minimum(x, 0.0) - jnp.log(1.0 + jnp.exp(-jnp.abs(x)))


def _sb_block(q, k, v, tri, carry, masked):
    z = lax.dot_general(q, k, (((1,), (1,)), ((), ())), preferred_element_type=f32)
    lk = _log_sigmoid(-z)
    if masked:
        row = lax.broadcasted_iota(jnp.int32, z.shape, 0)
        col = lax.broadcasted_iota(jnp.int32, z.shape, 1)
        mask = col < row
        lkm = jnp.where(mask, lk, 0.0)
    else:
        lkm = lk
    hi = lkm.astype(bf16)
    lo = (lkm - hi.astype(f32)).astype(bf16)
    suf = (jnp.dot(hi, tri, preferred_element_type=f32)
           + jnp.dot(lo, tri, preferred_element_type=f32))
    w = jnp.exp(z + lk + suf + carry)
    if masked:
        w = jnp.where(mask, w, 0.0)
    out = jnp.dot(w.astype(bf16), v, preferred_element_type=f32)
    total = jnp.sum(lkm, axis=1, keepdims=True)
    return out, carry + total


def _attn_kernel(*refs, tq, n_qblk, heads, tkp, n_past_blk):
    if n_past_blk:
        q_ref, k_ref, v_ref, pk_ref, pv_ref, tri_ref, o_ref, acc_s, car_s = refs
    else:
        q_ref, k_ref, v_ref, tri_ref, o_ref, acc_s, car_s = refs
    hd = HEAD_DIM

    def hs(g):
        return slice(g * hd, (g + 1) * hd)

    def sweep(qs, j0, load_kv):
        tri = tri_ref[...]

        def cond(st):
            j, mx = st
            return jnp.logical_and(j >= 0, mx > SB_EXIT)

        def body(st):
            j, _ = st
            mx = None
            for g in range(heads):
                kb, vb = load_kv(j, g)
                out, car = _sb_block(qs[g], kb, vb, tri, car_s[g], False)
                acc_s[g] = acc_s[g] + out
                car_s[g] = car
                m = jnp.max(car)
                mx = m if mx is None else jnp.maximum(mx, m)
            return j - 1, mx

        mx0 = None
        for g in range(heads):
            m = jnp.max(car_s[g])
            mx0 = m if mx0 is None else jnp.maximum(mx0, m)
        lax.while_loop(cond, body, (j0, mx0))

    def qblock(qb, _):
        r0 = pl.multiple_of(qb * tq, tq)
        qs = [q_ref[pl.ds(r0, tq), hs(g)] for g in range(heads)]
        tri_d = tri_ref[:tq, :tq]
        for g in range(heads):
            kd = k_ref[pl.ds(r0, tq), hs(g)].astype(bf16)
            vd = v_ref[pl.ds(r0, tq), hs(g)].astype(bf16)
            out, car = _sb_block(qs[g], kd, vd, tri_d, jnp.zeros((tq, tq), f32), True)
            acc_s[g] = out
            car_s[g] = jnp.broadcast_to(car[:, :1], (tq, LANES))

        def load_own(j, g):
            rj = pl.multiple_of(j * tq, tq)
            return (k_ref[pl.ds(rj, tq), hs(g)].astype(bf16),
                    v_ref[pl.ds(rj, tq), hs(g)].astype(bf16))

        if n_qblk > 1:
            sweep(qs, qb - 1, load_own)

        if n_past_blk:
            def load_past(j, g):
                rj = pl.multiple_of(j * tkp, tkp)
                return (pk_ref[pl.ds(rj, tkp), hs(g)].astype(bf16),
                        pv_ref[pl.ds(rj, tkp), hs(g)].astype(bf16))
            sweep(qs, n_past_blk - 1, load_past)

        for g in range(heads):
            o_ref[pl.ds(r0, tq), hs(g)] = acc_s[g].astype(o_ref.dtype)
        return 0

    if n_qblk == 1:
        qblock(0, 0)
    else:
        lax.fori_loop(0, n_qblk, qblock, 0)


def sb_attention(q, k, v, *, batch, seq, row0, k_stack=None, past_k=None, past_v=None,
                 out_buf=None, heads=4):
    n, width = q.shape
    tq = min(seq, LANES)
    n_qblk = seq // tq
    gw = heads * HEAD_DIM
    assert seq % tq == 0 and row0 % seq == 0 and width % gw == 0
    boff = row0 // seq
    tri = (jnp.arange(LANES)[:, None] > jnp.arange(LANES)[None, :]).astype(bf16)
    in_specs = [pl.BlockSpec((seq, gw), lambda b, h: (b + boff, h))]
    if k_stack is None:
        kv_spec = pl.BlockSpec((seq, gw), lambda b, h: (b, h))
    else:
        kv_spec = pl.BlockSpec((None, seq, gw), lambda b, h: (k_stack, b, h))
    in_specs += [kv_spec, kv_spec]
    args = [q, k, v]
    n_past_blk = 0
    if past_k is not None:
        past = past_k.shape[1]
        assert past % LANES == 0
        n_past_blk = past // LANES
        pspec = pl.BlockSpec((None, past, gw), lambda b, h: (b, 0, h))
        in_specs += [pspec, pspec]
        args += [past_k, past_v]
    in_specs.append(pl.BlockSpec((LANES, LANES), lambda b, h: (0, 0)))
    args.append(tri)
    aliases = {}
    if out_buf is not None:
        in_specs.append(pl.BlockSpec(memory_space=pl.ANY))
        args.append(out_buf)
        aliases = {len(args) - 1: 0}
    kern = functools.partial(_attn_kernel, tq=tq, n_qblk=n_qblk, heads=heads, tkp=LANES,
                             n_past_blk=n_past_blk)
    if out_buf is not None:
        kern = functools.partial(_drop_alias_ref, kern, len(args) - 1)
    return pl.pallas_call(
        kern,
        grid=(batch, width // gw),
        in_specs=in_specs,
        out_specs=pl.BlockSpec((seq, gw), lambda b, h: (b + boff, h)),
        out_shape=jax.ShapeDtypeStruct((n, width), bf16),
        scratch_shapes=[pltpu.VMEM((heads, tq, HEAD_DIM), f32),
                        pltpu.VMEM((heads, tq, LANES), f32)],
        input_output_aliases=aliases,
        compiler_params=_params(("parallel", "parallel")),
        name="sb_attention",
    )(*args)


def _neg_expm1(y):
    series = -y * (1.0 + y * (1.0 / 2 + y * (1.0 / 6 + y * (1.0 / 24 + y * (1.0 / 120 + y * (1.0 / 720))))))
    return jnp.where(y > -0.1, series, 1.0 - jnp.exp(y))


def _lru_kernel(xb_ref, yg_ref, cprev_ref, h0_ref, cw_ref, cb_ref, wa_ref, wx_ref, ba_ref, bx_ref,
                lam_ref, ob_ref, cnew_ref, hlast_ref, xpad_s, a_s, u_s, hs_s, h_s, *, tt, n_chunks):
    tc = pl.program_id(1)
    npad = SUBLANES
    nprev = CONV_W - 1

    @pl.when(tc == 0)
    def _():
        xpad_s[npad - nprev:npad, :] = cprev_ref[...]
        h_s[...] = h0_ref[...]

    @pl.when(tc > 0)
    def _():
        xpad_s[npad - nprev:npad, :] = xpad_s[npad + tt - nprev:npad + tt, :]

    xpad_s[npad:npad + tt, :] = xb_ref[...]
    xc = jnp.broadcast_to(cb_ref[...], (tt, cb_ref.shape[1]))
    for tap in range(CONV_W):
        s = npad - nprev + tap
        xc = xc + xpad_s[s:s + tt, :] * cw_ref[tap:tap + 1, :]

    log_sig_lam = _log_sigmoid(lam_ref[...])
    bw = LANES
    for nb in range(LRU_BLOCKS):
        cs = slice(nb * bw, (nb + 1) * bw)
        xcb = xc[:, cs]
        xcb16 = xcb.astype(bf16)
        r = jax.nn.sigmoid(jnp.dot(xcb16, wa_ref[nb], preferred_element_type=f32) + ba_ref[:, cs])
        i = jax.nn.sigmoid(jnp.dot(xcb16, wx_ref[nb], preferred_element_type=f32) + bx_ref[:, cs])
        log_a = (LRU_C * r) * log_sig_lam[:, cs]
        a_s[:, cs] = jnp.exp(log_a)
        u_s[:, cs] = jnp.sqrt(_neg_expm1(2.0 * log_a)) * (i * xcb)

    def step(t, h):
        h = a_s[pl.ds(t, 1), :] * h + u_s[pl.ds(t, 1), :]
        hs_s[pl.ds(t, 1), :] = h
        return h

    h = lax.fori_loop(0, tt, step, h_s[...], unroll=8)
    h_s[...] = h
    ob_ref[...] = (jax.nn.gelu(yg_ref[...].astype(f32)) * hs_s[...]).astype(ob_ref.dtype)

    @pl.when(tc == n_chunks - 1)
    def _():
        cnew_ref[...] = xpad_s[npad + tt - nprev:npad + tt, :]
        hlast_ref[...] = h


def rg_lru_mixer(xb, yg, conv_prev, h0, conv_w, conv_b, wa, wx, ba, bx, lam, *, batch, seq, row0,
                 out_buf=None):
    n, width = xb.shape
    tt = min(seq, 256)
    n_chunks = seq // tt
    assert seq % tt == 0 and row0 % tt == 0
    coff = row0 // tt
    row_spec = pl.BlockSpec((tt, width), lambda b, c: (coff + b * n_chunks + c, 0))
    vec = lambda r: pl.BlockSpec((r, width), lambda b, c: (0, 0))
    blk = pl.BlockSpec((LRU_BLOCKS, LANES, LANES), lambda b, c: (0, 0, 0))
    in_specs = [row_spec, row_spec,
                pl.BlockSpec((None, CONV_W - 1, width), lambda b, c: (b, 0, 0)),
                pl.BlockSpec((None, 1, width), lambda b, c: (b, 0, 0)),
                vec(CONV_W), vec(1), blk, blk, vec(1), vec(1), vec(1)]
    args = [xb, yg, conv_prev, h0.reshape(batch, 1, width), conv_w, conv_b.reshape(1, width),
            wa.astype(bf16), wx.astype(bf16), ba.reshape(1, width), bx.reshape(1, width),
            lam.reshape(1, width)]
    aliases = {}
    kern = functools.partial(_lru_kernel, tt=tt, n_chunks=n_chunks)
    if out_buf is not None:
        in_specs.append(pl.BlockSpec(memory_space=pl.ANY))
        args.append(out_buf)
        aliases = {len(args) - 1: 0}
        kern = functools.partial(_drop_alias_ref, kern, len(args) - 1)
    ob, cnew, hlast = pl.pallas_call(
        kern,
        grid=(batch, n_chunks),
        in_specs=in_specs,
        out_specs=[row_spec,
                   pl.BlockSpec((None, CONV_W - 1, width), lambda b, c: (b, 0, 0)),
                   pl.BlockSpec((None, 1, width), lambda b, c: (b, 0, 0))],
        out_shape=[jax.ShapeDtypeStruct((n, width), bf16),
                   jax.ShapeDtypeStruct((batch, CONV_W - 1, width), f32),
                   jax.ShapeDtypeStruct((batch, 1, width), f32)],
        scratch_shapes=[pltpu.VMEM((SUBLANES + tt, width), f32),
                        pltpu.VMEM((tt, width), f32),
                        pltpu.VMEM((tt, width), f32),
                        pltpu.VMEM((tt, width), f32),
                        pltpu.VMEM((1, width), f32)],
        input_output_aliases=aliases,
        compiler_params=_params(("parallel", "arbitrary")),
        name="rg_lru_mixer",
    )(*args)
    return ob, cnew, hlast.reshape(batch, width)


def _mix_kernel(oa_ref, ob_ref, ga_ref, gb_ref, wpa_ref, wpb_ref, o_ref, wa_s, wb_s):
    @pl.when(pl.program_id(1) == 0)
    def _():
        wa_s[...] = wpa_ref[...].astype(bf16)
        wb_s[...] = wpb_ref[...].astype(bf16)

    pa = jnp.dot(oa_ref[...], wa_s[...], preferred_element_type=f32)
    pb = jnp.dot(ob_ref[...], wb_s[...], preferred_element_type=f32)
    mixed = (jax.nn.sigmoid(ga_ref[...].astype(f32)) * pa
             + jax.nn.sigmoid(gb_ref[...].astype(f32)) * pb)
    o_ref[...] = mixed.astype(o_ref.dtype)


def gated_mix(oa, ob, ga, gb, w_pa, w_pb, layer):
    n, kdim = oa.shape
    ncols = w_pa.shape[2]
    tm = _pick(n, (768, 512, 256, 128, 64, 32))
    tn = _pick(ncols, (512, 256, 128))
    act = pl.BlockSpec((tm, kdim), lambda j, m: (m, 0))
    gate = pl.BlockSpec((tm, tn), lambda j, m: (m, j))
    wsp = pl.BlockSpec((None, kdim, tn), lambda j, m: (layer, 0, j))
    return pl.pallas_call(
        _mix_kernel,
        grid=(ncols // tn, n // tm),
        in_specs=[act, act, gate, gate, wsp, wsp],
        out_specs=gate,
        out_shape=jax.ShapeDtypeStruct((n, ncols), bf16),
        scratch_shapes=[pltpu.VMEM((kdim, tn), bf16), pltpu.VMEM((kdim, tn), bf16)],
        compiler_params=_params(("arbitrary", "arbitrary")),
        name="gated_mix",
    )(oa, ob, ga, gb, w_pa, w_pb)


def _router_kernel(x_ref, g_ref, wr_ref, br_ref, h_ref, route_ref):
    h = _rmsnorm_val(x_ref[...], g_ref[...]).astype(bf16)
    h_ref[...] = h
    logits = jnp.dot(h, wr_ref[...], preferred_element_type=f32) + br_ref[...]
    lane = lax.broadcasted_iota(jnp.int32, logits.shape, 1).astype(f32)
    neg = jnp.float32(-3.0e38)
    big = jnp.float32(LANES)

    def first_argmax(vals):
        m = jnp.max(vals, axis=1, keepdims=True)
        idx = jnp.min(jnp.where(vals == m, lane, big), axis=1, keepdims=True)
        return m, idx

    in_group_lanes = lane < N_GROUPS
    gl = jnp.where(in_group_lanes, logits, neg)
    gmax, gidx = first_argmax(gl)
    p_g = 1.0 / jnp.sum(jnp.where(in_group_lanes, jnp.exp(gl - gmax), 0.0), axis=1, keepdims=True)
    lo = ROUTE_EXPERT_LANE0 + EXPERTS_PER_GROUP * gidx
    sel = jnp.logical_and(lane >= lo, lane < lo + EXPERTS_PER_GROUP)
    sl = jnp.where(sel, logits, neg)
    m1, i1 = first_argmax(sl)
    sl2 = jnp.where(lane == i1, neg, sl)
    m2, i2 = first_argmax(sl2)
    e2 = jnp.exp(m2 - m1)
    den = 1.0 + e2
    w1 = (1.0 / den) * p_g
    w2 = (e2 / den) * p_g
    comb = jnp.where(lane == i1, w1, 0.0) + jnp.where(lane == i2, w2, 0.0)
    route_ref[...] = jnp.where(lane == ROUTE_GROUP_LANE, gidx, comb)


def router(x, g, wg, bg, ws, bs):
    n, d = x.shape
    tm = _pick(n, (256, 128, 64, 32))
    wsub = jnp.transpose(ws, (1, 0, 2)).reshape(d, N_EXPERTS)
    wr = jnp.concatenate([wg, wsub, jnp.zeros((d, LANES - N_GROUPS - N_EXPERTS), f32)], axis=1).astype(bf16)
    br = jnp.concatenate([bg, bs.reshape(-1), jnp.zeros((LANES - N_GROUPS - N_EXPERTS,), f32)]).reshape(1, LANES)
    return pl.pallas_call(
        _router_kernel,
        grid=(n // tm,),
        in_specs=[pl.BlockSpec((tm, d), lambda m: (m, 0)),
                  pl.BlockSpec((1, d), lambda m: (0, 0)),
                  pl.BlockSpec((d, LANES), lambda m: (0, 0)),
                  pl.BlockSpec((1, LANES), lambda m: (0, 0))],
        out_specs=[pl.BlockSpec((tm, d), lambda m: (m, 0)),
                   pl.BlockSpec((tm, LANES), lambda m: (m, 0))],
        out_shape=[jax.ShapeDtypeStruct((n, d), bf16),
                   jax.ShapeDtypeStruct((n, LANES), f32)],
        compiler_params=_params(("parallel",)),
        name="router",
    )(x, g.reshape(1, d), wr, br)


def _moe_kernel(h_ref, route_ref, wu_ref, wd_ref, x_ref, o_ref, *, rows, ff):
    e = pl.program_id(1)

    @pl.when(e == 0)
    def _():
        o_ref[...] = x_ref[...]

    tm = h_ref.shape[0]
    for r in range(tm // rows):
        rs = slice(r * rows, (r + 1) * rows)
        route = route_ref[rs, :]
        lane = lax.broadcasted_iota(jnp.int32, route.shape, 1)
        c = jnp.sum(jnp.where(lane == e + ROUTE_EXPERT_LANE0, route, 0.0), axis=1, keepdims=True)
        up = jnp.dot(h_ref[rs, :], wu_ref[...], preferred_element_type=f32)
        act = (jax.nn.silu(up[:, :ff]) * up[:, ff:]).astype(bf16)
        y = jnp.dot(act, wd_ref[...], preferred_element_type=f32)
        o_ref[rs, :] = o_ref[rs, :] + c * y


def moe_experts(h, route, wu, wd, x):
    n, d = h.shape
    n_exp, _, ff2 = wu.shape
    ff = ff2 // 2
    tm = _pick(n, (768, 512, 256, 128, 64, 32))
    rows = _pick(tm, (256, 128, 64, 32))
    return pl.pallas_call(
        functools.partial(_moe_kernel, rows=rows, ff=ff),
        grid=(n // tm, n_exp),
        in_specs=[pl.BlockSpec((tm, d), lambda m, e: (m, 0)),
                  pl.BlockSpec((tm, LANES), lambda m, e: (m, 0)),
                  pl.BlockSpec((None, d, ff2), lambda m, e: (e, 0, 0)),
                  pl.BlockSpec((None, ff, d), lambda m, e: (e, 0, 0)),
                  pl.BlockSpec((tm, d), lambda m, e: (m, 0))],
        out_specs=pl.BlockSpec((tm, d), lambda m, e: (m, 0)),
        out_shape=jax.ShapeDtypeStruct((n, d), f32),
        compiler_params=_params(("parallel", "arbitrary")),
        name="moe_experts",
    )(h, route, wu, wd, x)


def kernel(x_prompt, x_sample, cache_k, cache_v, state_conv, state_lru, norm1_g, w_in, conv_w, conv_b,
           lru_wa, lru_ba, lru_wx, lru_bx, lru_lambda, w_pa, w_pb, w_o, norm2_g, router_group_w,
           router_group_b, router_sub_w, router_sub_b, expert_w_up, expert_w_down, final_norm_g):
    bp, tp, d = x_prompt.shape
    bs, ts, _ = x_sample.shape
    depth = w_in.shape[0]
    past = cache_k.shape[2]
    sbw = N_HEADS * HEAD_DIM
    lw = conv_w.shape[2]
    n_p, n_s = bp * tp, bs * ts
    n = n_p + n_s

    x = jnp.concatenate([x_prompt.reshape(n_p, d), x_sample.reshape(n_s, d)], axis=0)
    cache_k = cache_k.reshape(depth, bs, past, sbw)
    cache_v = cache_v.reshape(depth, bs, past, sbw)
    zero_conv = jnp.zeros((bp, CONV_W - 1, lw), f32)
    zero_lru = jnp.zeros((bp, lw), f32)

    kp = vp = None
    ks_l, vs_l, cp_l, cs_l, hp_l, hs_l = [], [], [], [], [], []
    for l in range(depth):
        h1 = rmsnorm(x, norm1_g[l], bf16)
        q = matmul(h1, w_in, l, 0, sbw, bf16, scale=SB_SCALE)
        kp = matmul(h1, w_in, l, sbw, sbw, f32, nrows=n_p, out_stack=(depth, l), out_buf=kp)
        vp = matmul(h1, w_in, l, 2 * sbw, sbw, f32, nrows=n_p, out_stack=(depth, l), out_buf=vp)
        ks = matmul(h1, w_in, l, sbw, sbw, f32, row0=n_p, nrows=n_s)
        vs = matmul(h1, w_in, l, 2 * sbw, sbw, f32, row0=n_p, nrows=n_s)
        c0 = 3 * sbw
        xb = matmul(h1, w_in, l, c0, lw, f32)
        yg = matmul(h1, w_in, l, c0 + lw, lw, bf16)
        ga = matmul(h1, w_in, l, c0 + 2 * lw, d, bf16)
        gb = matmul(h1, w_in, l, c0 + 2 * lw + d, d, bf16)

        oa = sb_attention(q, kp, vp, batch=bp, seq=tp, row0=0, k_stack=l)
        oa = sb_attention(q, ks, vs, batch=bs, seq=ts, row0=n_p, past_k=cache_k[l], past_v=cache_v[l],
                          out_buf=oa)

        lru_args = (conv_w[l], conv_b[l], lru_wa[l], lru_wx[l], lru_ba[l], lru_bx[l], lru_lambda[l])
        ob, cp, hp = rg_lru_mixer(xb, yg, zero_conv, zero_lru, *lru_args, batch=bp, seq=tp, row0=0)
        ob, cs, hs = rg_lru_mixer(xb, yg, state_conv[l], state_lru[l], *lru_args, batch=bs, seq=ts,
                                  row0=n_p, out_buf=ob)

        mixed = gated_mix(oa, ob, ga, gb, w_pa, w_pb, l)
        x1 = matmul(mixed, w_o, l, 0, d, f32, res=x)
        h2, route = router(x1, norm2_g[l], router_group_w[l], router_group_b[l], router_sub_w[l],
                           router_sub_b[l])
        x = moe_experts(h2, route, expert_w_up[l].astype(bf16), expert_w_down[l].astype(bf16), x1)

        ks_l.append(ks)
        vs_l.append(vs)
        cp_l.append(cp)
        cs_l.append(cs)
        hp_l.append(hp)
        hs_l.append(hs)

    y_prompt = rmsnorm(x, final_norm_g, f32, row0=0, nrows=n_p).reshape(bp, tp, d)
    y_sample = rmsnorm(x, final_norm_g, f32, row0=n_p, nrows=n_s).reshape(bs, ts, d)
    prompt_k = kp.reshape(depth, bp, tp, N_HEADS, HEAD_DIM)
    prompt_v = vp.reshape(depth, bp, tp, N_HEADS, HEAD_DIM)
    sample_k = jnp.stack(ks_l).reshape(depth, bs, ts, N_HEADS, HEAD_DIM)
    sample_v = jnp.stack(vs_l).reshape(depth, bs, ts, N_HEADS, HEAD_DIM)
    return (y_prompt, y_sample, prompt_k, prompt_v, jnp.stack(cp_l), jnp.stack(hp_l),
            sample_k, sample_v, jnp.stack(cs_l), jnp.stack(hs_l))
```

```python
import functools

import jax
import jax.numpy as jnp
from jax import lax
from jax.experimental import pallas as pl
from jax.experimental.pallas import tpu as pltpu

N_HEADS = 16
HEAD_DIM = 128
SB_SCALE = HEAD_DIM ** -0.5
LRU_BLOCKS = 16
CONV_W = 4
LRU_C = 8.0
N_GROUPS = 4
EXPERTS_PER_GROUP = 4
N_EXPERTS = N_GROUPS * EXPERTS_PER_GROUP
NORM_EPS = 1e-6

LANES = 128
SUBLANES = 8
VMEM_LIMIT = 56 * 1024 * 1024
SB_EXIT = -110.0
ATTN_HEADS = 8
ROUTE_EXPERT_LANE0 = 4
MOE_TILE = 512

f32 = jnp.float32
bf16 = jnp.bfloat16


def _pick(n, cands):
    for c in cands:
        if n % c == 0:
            return c
    raise ValueError(f"no tile size for {n}")


def _params(sem):
    return pltpu.CompilerParams(dimension_semantics=sem, vmem_limit_bytes=VMEM_LIMIT)


def _drop_ref(kern, pos, *refs):
    kern(*(refs[:pos] + refs[pos + 1:]))


def _log_sigmoid(x):
    return jnp.minimum(x, 0.0) - jnp.log(1.0 + jnp.exp(-jnp.abs(x)))


def _slab_store(ref, val):
    tm, d = val.shape
    s_per = d // LANES
    for s in range(s_per):
        ref[pl.ds(s, tm, stride=s_per), :] = val[:, s * LANES:(s + 1) * LANES]


def _slab_load(ref, tm, s_per):
    return jnp.concatenate([ref[pl.ds(s, tm, stride=s_per), :] for s in range(s_per)], axis=1)


def _rmsnorm_val(x, g):
    ms = jnp.mean(x * x, axis=-1, keepdims=True)
    return (x * lax.rsqrt(ms + NORM_EPS)) * g


def _rmsnorm_kernel(x_ref, g_ref, o_ref):
    o_ref[...] = _rmsnorm_val(x_ref[...], g_ref[...]).astype(o_ref.dtype)


def rmsnorm(x, g, out_dtype):
    n, d = x.shape
    tm = _pick(n, (512, 256, 128, 64, 32, 16, 8))
    return pl.pallas_call(
        _rmsnorm_kernel,
        grid=(n // tm,),
        in_specs=[pl.BlockSpec((tm, d), lambda m: (m, 0)),
                  pl.BlockSpec((1, d), lambda m: (0, 0))],
        out_specs=pl.BlockSpec((tm, d), lambda m: (m, 0)),
        out_shape=jax.ShapeDtypeStruct((n, d), out_dtype),
        compiler_params=_params(("parallel",)),
        name="rmsnorm",
    )(x, g.reshape(1, d))


def _mm_kernel(*refs, scale, has_res):
    if has_res:
        h_ref, w_ref, r_ref, o_ref, wb_ref = refs
    else:
        h_ref, w_ref, o_ref, wb_ref = refs

    @pl.when(pl.program_id(1) == 0)
    def _():
        wb_ref[...] = w_ref[...].astype(bf16)

    acc = jnp.dot(h_ref[...], wb_ref[...], preferred_element_type=f32)
    if scale != 1.0:
        acc = acc * scale
    if has_res:
        acc = acc + r_ref[...]
    o_ref[...] = acc.astype(o_ref.dtype)


def matmul(h, w, layer, col0, ncols, out_dtype, *, scale=1.0, res=None, row0=0, nrows=None):
    n, kdim = h.shape
    nrows = n - row0 if nrows is None else nrows
    tm = _pick(nrows, (768, 512, 256, 128, 64, 32))
    tn = _pick(ncols, (1024, 512, 256, 128))
    assert row0 % tm == 0 and col0 % tn == 0
    roff, coff = row0 // tm, col0 // tn
    in_specs = [pl.BlockSpec((tm, kdim), lambda j, m: (m + roff, 0)),
                pl.BlockSpec((None, kdim, tn), lambda j, m: (layer, 0, j + coff))]
    args = [h, w]
    if res is not None:
        assert row0 == 0 and col0 == 0
        in_specs.append(pl.BlockSpec((tm, tn), lambda j, m: (m, j)))
        args.append(res)
    return pl.pallas_call(
        functools.partial(_mm_kernel, scale=scale, has_res=res is not None),
        grid=(ncols // tn, nrows // tm),
        in_specs=in_specs,
        out_specs=pl.BlockSpec((tm, tn), lambda j, m: (m, j)),
        out_shape=jax.ShapeDtypeStruct((nrows, ncols), out_dtype),
        scratch_shapes=[pltpu.VMEM((kdim, tn), bf16)],
        compiler_params=_params(("arbitrary", "arbitrary")),
        name="matmul",
    )(*args)


def _kv_kernel(h_ref, w_ref, o5_ref, ob_ref, wb_ref, *, layer, fill):
    @pl.when(jnp.logical_and(pl.program_id(1) == 0, pl.program_id(2) == 0))
    def _():
        wb_ref[...] = w_ref[...].astype(bf16)

    def project():
        acc = jnp.dot(h_ref[...], wb_ref[...], preferred_element_type=f32)
        ob_ref[...] = acc.astype(bf16)
        for hh in range(o5_ref.shape[1]):
            o5_ref[:, hh, :] = acc[:, hh * HEAD_DIM:(hh + 1) * HEAD_DIM]

    if fill:
        pl.when(pl.program_id(2) == layer)(project)

        @pl.when(pl.program_id(2) != layer)
        def _():
            o5_ref[...] = jnp.zeros_like(o5_ref)
    else:
        project()


def kv_matmul(h, w, layer, col0, nrows, depth, out_buf):
    n, kdim = h.shape
    ncols = N_HEADS * HEAD_DIM
    tm = _pick(nrows, (512, 256, 128, 64, 32))
    hpt = SUBLANES
    tn = hpt * HEAD_DIM
    coff = col0 // tn
    fill = out_buf is None
    in_specs = [pl.BlockSpec((tm, kdim), lambda j, m, s: (m, 0)),
                pl.BlockSpec((None, kdim, tn), lambda j, m, s: (layer, 0, j + coff))]
    args = [h, w]
    kern = functools.partial(_kv_kernel, layer=layer, fill=fill)
    aliases = {}
    if fill:
        slab = lambda s: s
    else:
        slab = lambda s: layer
        in_specs.append(pl.BlockSpec(memory_space=pl.ANY))
        args.append(out_buf)
        aliases = {2: 0}
        kern = functools.partial(_drop_ref, kern, 2)
    return pl.pallas_call(
        kern,
        grid=(ncols // tn, nrows // tm, depth if fill else 1),
        in_specs=in_specs,
        out_specs=[pl.BlockSpec((None, tm, hpt, HEAD_DIM), lambda j, m, s: (slab(s), m, j, 0)),
                   pl.BlockSpec((tm, tn), lambda j, m, s: (m, j))],
        out_shape=[jax.ShapeDtypeStruct((depth, nrows, N_HEADS, HEAD_DIM), f32),
                   jax.ShapeDtypeStruct((nrows, ncols), bf16)],
        scratch_shapes=[pltpu.VMEM((kdim, tn), bf16)],
        input_output_aliases=aliases,
        compiler_params=_params(("arbitrary", "arbitrary", "arbitrary")),
        name="kv_matmul",
    )(*args)


def _pair_blockdiag(x2):
    lane = lax.broadcasted_iota(jnp.int32, x2.shape, 1)
    first = lane < HEAD_DIM
    zero = jnp.zeros_like(x2)
    return jnp.concatenate([jnp.where(first, x2, zero), jnp.where(first, zero, x2)], axis=0)


def _pair_cols(t0, t1, tq):
    return jnp.concatenate([jnp.broadcast_to(t0, (tq, LANES)), jnp.broadcast_to(t1, (tq, LANES))], axis=1)


def _sb_pair_block(q2, k2, v2, tri2, carry2, masked):
    tk = k2.shape[0]
    z = lax.dot_general(q2, _pair_blockdiag(k2), (((1,), (1,)), ((), ())), preferred_element_type=f32)
    lk = _log_sigmoid(-z)
    if masked:
        row = lax.broadcasted_iota(jnp.int32, z.shape, 0)
        col = lax.broadcasted_iota(jnp.int32, z.shape, 1)
        mask = jnp.where(col >= tk, col - tk, col) < row
        lkm = jnp.where(mask, lk, 0.0)
    else:
        lkm = lk
    hi = lkm.astype(bf16)
    lo = (lkm - hi.astype(f32)).astype(bf16)
    suf = (jnp.dot(hi, tri2, preferred_element_type=f32)
           + jnp.dot(lo, tri2, preferred_element_type=f32))
    arg = z + lk + suf
    if carry2 is not None:
        arg = arg + carry2
    w = jnp.exp(arg)
    if masked:
        w = jnp.where(mask, w, 0.0)
    out = jnp.dot(w.astype(bf16), _pair_blockdiag(v2), preferred_element_type=f32)
    t0 = jnp.sum(lkm[:, :tk], axis=1, keepdims=True)
    t1 = jnp.sum(lkm[:, tk:], axis=1, keepdims=True)
    return out, t0, t1


def _attn_kernel(*refs, tq, n_qblk, pairs, n_past_blk, layer):
    refs = list(refs)
    q_ref, k_ref, v_ref = refs[:3]
    pos = 3
    if n_past_blk:
        pk_hbm, pv_hbm = refs[3:5]
        pos = 5
    tri_ref = refs[pos]
    pos += 1
    trid_ref = tri_ref
    if tq != LANES:
        trid_ref = refs[pos]
        pos += 1
    o_ref = refs[pos]
    pos += 1
    acc = refs[pos:pos + pairs]
    car = refs[pos + pairs:pos + 2 * pairs]
    pos += 2 * pairs
    if n_past_blk:
        kbuf, vbuf, sem = refs[pos:pos + 3]
    pw = 2 * HEAD_DIM
    heads = 2 * pairs

    def ps(p):
        return slice(p * pw, (p + 1) * pw)

    def sweep(q2s, j0, load_kv, before=None):
        tri2 = tri_ref[...]

        def cond(st):
            j, mx = st
            return jnp.logical_and(j >= 0, mx > SB_EXIT)

        def body(st):
            j, _ = st
            if before is not None:
                before(j)
            mx = None
            for p in range(pairs):
                k2, v2 = load_kv(j, p)
                c2 = car[p][...]
                out, t0, t1 = _sb_pair_block(q2s[p], k2, v2, tri2, c2, False)
                acc[p][...] = acc[p][...] + out
                c2 = c2 + _pair_cols(t0, t1, tq)
                car[p][...] = c2
                m = jnp.max(c2)
                mx = m if mx is None else jnp.maximum(mx, m)
            return j - 1, mx

        mx0 = None
        for p in range(pairs):
            m = jnp.max(car[p][...])
            mx0 = m if mx0 is None else jnp.maximum(mx0, m)
        return lax.while_loop(cond, body, (j0, mx0))

    if n_past_blk:
        b = pl.program_id(0)
        h0 = pl.multiple_of(pl.program_id(1) * heads, heads)

        def cache_copies(j, slot):
            rows = pl.ds(pl.multiple_of(j * LANES, LANES), LANES)
            return (pltpu.make_async_copy(pk_hbm.at[layer, b, rows, pl.ds(h0, heads), :], kbuf.at[slot],
                                          sem.at[0, slot]),
                    pltpu.make_async_copy(pv_hbm.at[layer, b, rows, pl.ds(h0, heads), :], vbuf.at[slot],
                                          sem.at[1, slot]))

        def cache_start(j, slot):
            for c in cache_copies(j, slot):
                c.start()

        def cache_wait(j, slot):
            for c in cache_copies(j, slot):
                c.wait()

    def qblock(qb, _):
        r0 = pl.multiple_of(qb * tq, tq)
        if n_past_blk:
            cache_start(n_past_blk - 1, (n_past_blk - 1) % 2)
        q2s = [q_ref[pl.ds(r0, tq), ps(p)] for p in range(pairs)]
        trid = trid_ref[...]
        for p in range(pairs):
            kd = k_ref[pl.ds(r0, tq), ps(p)].astype(bf16)
            vd = v_ref[pl.ds(r0, tq), ps(p)].astype(bf16)
            out, t0, t1 = _sb_pair_block(q2s[p], kd, vd, trid, None, True)
            acc[p][...] = out
            car[p][...] = _pair_cols(t0, t1, tq)

        if n_qblk > 1:
            def load_own(j, p):
                rj = pl.multiple_of(j * tq, tq)
                return (k_ref[pl.ds(rj, tq), ps(p)].astype(bf16),
                        v_ref[pl.ds(rj, tq), ps(p)].astype(bf16))
            sweep(q2s, qb - 1, load_own)

        if n_past_blk:
            def before(j):
                cache_wait(j, j % 2)

                @pl.when(j >= 1)
                def _():
                    cache_start(j - 1, (j - 1) % 2)

            def load_past(j, p):
                kb = kbuf.at[j % 2]
                vb = vbuf.at[j % 2]
                k2 = jnp.concatenate([kb[:, 2 * p, :], kb[:, 2 * p + 1, :]], axis=1)
                v2 = jnp.concatenate([vb[:, 2 * p, :], vb[:, 2 * p + 1, :]], axis=1)
                return k2.astype(bf16), v2.astype(bf16)

            j_end, _ = sweep(q2s, n_past_blk - 1, load_past, before)

            @pl.when(j_end >= 0)
            def _():
                cache_wait(j_end, j_end % 2)

        for p in range(pairs):
            o_ref[pl.ds(r0, tq), ps(p)] = acc[p][...].astype(o_ref.dtype)
        return 0

    if n_qblk == 1:
        qblock(0, 0)
    else:
        lax.fori_loop(0, n_qblk, qblock, 0)


def _tri_pair(t):
    tri = (jnp.arange(t)[:, None] > jnp.arange(t)[None, :]).astype(bf16)
    z = jnp.zeros((t, t), bf16)
    return jnp.concatenate([jnp.concatenate([tri, z], axis=1), jnp.concatenate([z, tri], axis=1)], axis=0)


def sb_attention(q, k, v, *, batch, seq, row0, layer=0, past_k=None, past_v=None):
    n, width = q.shape
    tq = min(seq, LANES)
    n_qblk = seq // tq
    heads = ATTN_HEADS
    pairs = heads // 2
    gw = heads * HEAD_DIM
    assert seq % tq == 0 and row0 % seq == 0 and width % gw == 0
    boff = row0 // seq
    in_specs = [pl.BlockSpec((seq, gw), lambda b, h: (b + boff, h)),
                pl.BlockSpec((seq, gw), lambda b, h: (b, h)),
                pl.BlockSpec((seq, gw), lambda b, h: (b, h))]
    args = [q, k, v]
    scratch = ([pltpu.VMEM((tq, 2 * HEAD_DIM), f32) for _ in range(pairs)]
               + [pltpu.VMEM((tq, 2 * LANES), f32) for _ in range(pairs)])
    n_past_blk = 0
    if past_k is not None:
        past = past_k.shape[2]
        assert past % LANES == 0
        n_past_blk = past // LANES
        in_specs += [pl.BlockSpec(memory_space=pl.ANY), pl.BlockSpec(memory_space=pl.ANY)]
        args += [past_k, past_v]
        scratch += [pltpu.VMEM((2, LANES, heads, HEAD_DIM), f32),
                    pltpu.VMEM((2, LANES, heads, HEAD_DIM), f32),
                    pltpu.SemaphoreType.DMA((2, 2))]
    in_specs.append(pl.BlockSpec((2 * LANES, 2 * LANES), lambda b, h: (0, 0)))
    args.append(_tri_pair(LANES))
    if tq != LANES:
        in_specs.append(pl.BlockSpec((2 * tq, 2 * tq), lambda b, h: (0, 0)))
        args.append(_tri_pair(tq))
    kern = functools.partial(_attn_kernel, tq=tq, n_qblk=n_qblk, pairs=pairs, n_past_blk=n_past_blk,
                             layer=layer)
    return pl.pallas_call(
        kern,
        grid=(batch, width // gw),
        in_specs=in_specs,
        out_specs=pl.BlockSpec((seq, gw), lambda b, h: (b, h)),
        out_shape=jax.ShapeDtypeStruct((batch * seq, width), bf16),
        scratch_shapes=scratch,
        compiler_params=_params(("parallel", "parallel")),
        name="sb_attention",
    )(*args)


def _neg_expm1(y):
    series = -y * (1.0 + y * (1.0 / 2 + y * (1.0 / 6 + y * (1.0 / 24 + y * (1.0 / 120 + y * (1.0 / 720))))))
    return jnp.where(y > -0.1, series, 1.0 - jnp.exp(y))


def _lru_kernel(xb_ref, yg_ref, cprev_ref, h0_ref, cw_ref, cb_ref, wa_ref, wx_ref, ba_ref, bx_ref,
                lam_ref, ob_ref, cnew_ref, hlast_ref, xpad_s, a_s, u_s, hs_s, h_s, *, tt, n_chunks):
    tc = pl.program_id(1)
    npad = SUBLANES
    nprev = CONV_W - 1

    @pl.when(tc == 0)
    def _():
        xpad_s[npad - nprev:npad, :] = cprev_ref[...]
        h_s[...] = h0_ref[...]

    @pl.when(tc > 0)
    def _():
        xpad_s[npad - nprev:npad, :] = xpad_s[npad + tt - nprev:npad + tt, :]

    xpad_s[npad:npad + tt, :] = xb_ref[...]
    xc = jnp.broadcast_to(cb_ref[...], (tt, cb_ref.shape[1]))
    for tap in range(CONV_W):
        s = npad - nprev + tap
        xc = xc + xpad_s[s:s + tt, :] * cw_ref[tap:tap + 1, :]

    log_sig_lam = _log_sigmoid(lam_ref[...])
    bw = LANES
    for nb in range(LRU_BLOCKS):
        cs = slice(nb * bw, (nb + 1) * bw)
        xcb = xc[:, cs]
        xcb16 = xcb.astype(bf16)
        r = jax.nn.sigmoid(jnp.dot(xcb16, wa_ref[nb], preferred_element_type=f32) + ba_ref[:, cs])
        i = jax.nn.sigmoid(jnp.dot(xcb16, wx_ref[nb], preferred_element_type=f32) + bx_ref[:, cs])
        log_a = (LRU_C * r) * log_sig_lam[:, cs]
        a_s[:, cs] = jnp.exp(log_a)
        u_s[:, cs] = jnp.sqrt(_neg_expm1(2.0 * log_a)) * (i * xcb)

    def step(t, h):
        h = a_s[pl.ds(t, 1), :] * h + u_s[pl.ds(t, 1), :]
        hs_s[pl.ds(t, 1), :] = h
        return h

    h = lax.fori_loop(0, tt, step, h_s[...], unroll=8)
    h_s[...] = h
    ob_ref[...] = (jax.nn.gelu(yg_ref[...].astype(f32)) * hs_s[...]).astype(ob_ref.dtype)

    @pl.when(tc == n_chunks - 1)
    def _():
        cnew_ref[...] = xpad_s[npad + tt - nprev:npad + tt, :]
        hlast_ref[...] = h


def rg_lru_mixer(xb, yg, conv_prev, h0, conv_w, conv_b, wa, wx, ba, bx, lam, *, batch, seq, row0):
    n, width = xb.shape
    tt = min(seq, 256)
    n_chunks = seq // tt
    assert seq % tt == 0 and row0 % tt == 0
    coff = row0 // tt
    row_spec = pl.BlockSpec((tt, width), lambda b, c: (coff + b * n_chunks + c, 0))
    out_row_spec = pl.BlockSpec((tt, width), lambda b, c: (b * n_chunks + c, 0))
    vec = lambda r: pl.BlockSpec((r, width), lambda b, c: (0, 0))
    blk = pl.BlockSpec((LRU_BLOCKS, LANES, LANES), lambda b, c: (0, 0, 0))
    in_specs = [row_spec, row_spec,
                pl.BlockSpec((None, CONV_W - 1, width), lambda b, c: (b, 0, 0)),
                pl.BlockSpec((None, 1, width), lambda b, c: (b, 0, 0)),
                vec(CONV_W), vec(1), blk, blk, vec(1), vec(1), vec(1)]
    args = [xb, yg, conv_prev, h0.reshape(batch, 1, width), conv_w, conv_b.reshape(1, width),
            wa.astype(bf16), wx.astype(bf16), ba.reshape(1, width), bx.reshape(1, width),
            lam.reshape(1, width)]
    ob, cnew, hlast = pl.pallas_call(
        functools.partial(_lru_kernel, tt=tt, n_chunks=n_chunks),
        grid=(batch, n_chunks),
        in_specs=in_specs,
        out_specs=[out_row_spec,
                   pl.BlockSpec((None, CONV_W - 1, width), lambda b, c: (b, 0, 0)),
                   pl.BlockSpec((None, 1, width), lambda b, c: (b, 0, 0))],
        out_shape=[jax.ShapeDtypeStruct((batch * seq, width), bf16),
                   jax.ShapeDtypeStruct((batch, CONV_W - 1, width), f32),
                   jax.ShapeDtypeStruct((batch, 1, width), f32)],
        scratch_shapes=[pltpu.VMEM((SUBLANES + tt, width), f32),
                        pltpu.VMEM((tt, width), f32),
                        pltpu.VMEM((tt, width), f32),
                        pltpu.VMEM((tt, width), f32),
                        pltpu.VMEM((1, width), f32)],
        compiler_params=_params(("parallel", "arbitrary")),
        name="rg_lru_mixer",
    )(*args)
    return ob, cnew, hlast.reshape(batch, width)


def _mix_kernel(oap_ref, oas_ref, obp_ref, obs_ref, ga_ref, gb_ref, wpa_ref, wpb_ref, o_ref, wa_s, wb_s,
                *, prompt_tiles):
    m = pl.program_id(1)

    @pl.when(m == 0)
    def _():
        wa_s[...] = wpa_ref[...].astype(bf16)
        wb_s[...] = wpb_ref[...].astype(bf16)

    def mix(oa_ref, ob_ref):
        pa = jnp.dot(oa_ref[...], wa_s[...], preferred_element_type=f32)
        pb = jnp.dot(ob_ref[...], wb_s[...], preferred_element_type=f32)
        mixed = (jax.nn.sigmoid(ga_ref[...].astype(f32)) * pa
                 + jax.nn.sigmoid(gb_ref[...].astype(f32)) * pb)
        o_ref[...] = mixed.astype(o_ref.dtype)

    pl.when(m < prompt_tiles)(functools.partial(mix, oap_ref, obp_ref))
    pl.when(m >= prompt_tiles)(functools.partial(mix, oas_ref, obs_ref))


def gated_mix(oa_p, oa_s, ob_p, ob_s, ga, gb, w_pa, w_pb, layer):
    n_p, kdim = oa_p.shape
    n_s = oa_s.shape[0]
    ncols = w_pa.shape[2]
    tm = _pick(n_s, (256, 128, 64, 32))
    assert n_p % tm == 0
    pt = n_p // tm
    tn = _pick(ncols, (1024, 512, 256, 128))
    act_p =pl.BlockSpec((tm, kdim), lambda j, m: (jnp.minimum(m, pt - 1), 0))
    act_s = pl.BlockSpec((tm, kdim), lambda j, m: (jnp.maximum(m - pt, 0), 0))
    gate = pl.BlockSpec((tm, tn), lambda j, m: (m, j))
    wsp = pl.BlockSpec((None, kdim, tn), lambda j, m: (layer, 0, j))
    return pl.pallas_call(
        functools.partial(_mix_kernel, prompt_tiles=pt),
        grid=(ncols // tn, (n_p + n_s) // tm),
        in_specs=[act_p, act_s, act_p, act_s, gate, gate, wsp, wsp],
        out_specs=gate,
        out_shape=jax.ShapeDtypeStruct((n_p + n_s, ncols), bf16),
        scratch_shapes=[pltpu.VMEM((kdim, tn), bf16), pltpu.VMEM((kdim, tn), bf16)],
        compiler_params=_params(("arbitrary", "arbitrary")),
        name="gated_mix",
    )(oa_p, oa_s, ob_p, ob_s, ga, gb, w_pa, w_pb)


def _route(logits, forced_group=None):
    lane = lax.broadcasted_iota(jnp.int32, logits.shape, 1).astype(f32)
    neg = jnp.float32(-3.0e38)
    big = jnp.float32(LANES)

    def first_argmax(vals):
        m = jnp.max(vals, axis=1, keepdims=True)
        idx = jnp.min(jnp.where(vals == m, lane, big), axis=1, keepdims=True)
        return m, idx

    group_lanes = lane < N_GROUPS
    gl = jnp.where(group_lanes, logits, neg)
    gmax, gidx = first_argmax(gl)
    if forced_group is None:
        g_logit = gmax
    else:
        gidx = jnp.full_like(gidx, forced_group)
        g_logit = jnp.sum(jnp.where(lane == gidx, logits, 0.0), axis=1, keepdims=True)
    p_g = jnp.exp(g_logit - gmax) / jnp.sum(jnp.where(group_lanes, jnp.exp(gl - gmax), 0.0), axis=1,
                                             keepdims=True)
    lo = ROUTE_EXPERT_LANE0 + EXPERTS_PER_GROUP * gidx
    sel = jnp.logical_and(lane >= lo, lane < lo + EXPERTS_PER_GROUP)
    sl = jnp.where(sel, logits, neg)
    m1, i1 = first_argmax(sl)
    sl2 = jnp.where(lane == i1, neg, sl)
    m2, i2 = first_argmax(sl2)
    e2 = jnp.exp(m2 - m1)
    den = 1.0 + e2
    w1 = (1.0 / den) * p_g
    w2 = (e2 / den) * p_g
    comb = jnp.where(lane == i1, w1, 0.0) + jnp.where(lane == i2, w2, 0.0)
    return gidx, comb


def _router_weights(wg, bg, ws, bs):
    d = wg.shape[0]
    pad = LANES - N_GROUPS - N_EXPERTS
    wsub = jnp.transpose(ws, (1, 0, 2)).reshape(d, N_EXPERTS)
    wr = jnp.concatenate([wg, wsub, jnp.zeros((d, pad), f32)], axis=1).astype(bf16)
    br = jnp.concatenate([bg, bs.reshape(-1), jnp.zeros((pad,), f32)]).reshape(1, LANES)
    return wr, br


def _router_kernel(x_ref, g_ref, wr_ref, br_ref, slab_ref, grp_ref):
    h = _rmsnorm_val(x_ref[...], g_ref[...]).astype(bf16)
    logits = jnp.dot(h, wr_ref[...], preferred_element_type=f32) + br_ref[...]
    gidx, _ = _route(logits)
    grp_ref[...] = jnp.broadcast_to(gidx, grp_ref.shape)
    _slab_store(slab_ref, h.astype(f32))


def router(x, g, wr, br):
    n, d = x.shape
    tm = _pick(n, (256, 128, 64, 32))
    s_per = d // LANES
    return pl.pallas_call(
        _router_kernel,
        grid=(n // tm,),
        in_specs=[pl.BlockSpec((tm, d), lambda m: (m, 0)),
                  pl.BlockSpec((1, d), lambda m: (0, 0)),
                  pl.BlockSpec((d, LANES), lambda m: (0, 0)),
                  pl.BlockSpec((1, LANES), lambda m: (0, 0))],
        out_specs=[pl.BlockSpec((tm * s_per, LANES), lambda m: (m, 0)),
                   pl.BlockSpec((tm, LANES), lambda m: (m, 0))],
        out_shape=[jax.ShapeDtypeStruct((n * s_per, LANES), f32),
                   jax.ShapeDtypeStruct((n, LANES), f32)],
        compiler_params=_params(("parallel",)),
        name="router",
    )(x, g.reshape(1, d), wr, br)


def _group_layout(grp, tm):
    n = grp.shape[0]
    g = grp[:, 0].astype(jnp.int32)
    onehot = (g[:, None] == jnp.arange(N_GROUPS, dtype=jnp.int32)[None, :]).astype(jnp.int32)
    csum = jnp.cumsum(onehot, axis=0)
    counts = csum[-1]
    rank = jnp.sum(onehot * csum, axis=1) - 1
    tiles = (counts + tm - 1) // tm
    tile_end = jnp.cumsum(tiles)
    tile_start = tile_end - tiles
    pos = (jnp.sum(onehot * tile_start[None, :], axis=1) * tm + rank).astype(jnp.int32)
    nt_max = (n + N_GROUPS * (tm - 1)) // tm
    src = jnp.zeros((nt_max * tm,), jnp.int32).at[pos].set(jnp.arange(n, dtype=jnp.int32))
    tix = jnp.arange(nt_max, dtype=jnp.int32)
    tile_group = jnp.minimum(jnp.sum((tix[:, None] >= tile_end[None, :]).astype(jnp.int32), axis=1),
                             N_GROUPS - 1).astype(jnp.int32)
    return src, pos, tile_group, tile_end[-1:].astype(jnp.int32)


def _moe_kernel(src_ref, tg_ref, nt_ref, slab_hbm, wr_ref, br_ref, wu_ref, wd_ref, ys_ref,
                gbuf, sem, lhs_s, comb_s, acc_s, *, tm, s_per, ff, rows):
    t = pl.program_id(0)
    e = pl.program_id(1)
    nt = nt_ref[0]

    def row_copy(row, i, slot):
        return pltpu.make_async_copy(slab_hbm.at[pl.ds(pl.multiple_of(row * s_per, s_per), s_per), :],
                                     gbuf.at[slot, pl.ds(pl.multiple_of(i * s_per, s_per), s_per), :],
                                     sem.at[slot])

    def issue(tile, slot):
        def body(i, _):
            row_copy(src_ref[tile * tm + i], i, slot).start()
            return 0
        lax.fori_loop(0, tm, body, 0)

    def wait_tile(slot):
        def body(i, _):
            row_copy(0, i, slot).wait()
            return 0
        lax.fori_loop(0, tm, body, 0)

    @pl.when(jnp.logical_and(e == 0, t < nt))
    def _():
        @pl.when(t == 0)
        def _():
            issue(0, 0)

        @pl.when(t + 1 < nt)
        def _():
            issue(t + 1, (t + 1) % 2)

        wait_tile(t % 2)
        tile = gbuf.at[t % 2]
        for s in range(s_per):
            lhs_s[:, s * LANES:(s + 1) * LANES] = tile[pl.ds(s, tm, stride=s_per), :].astype(bf16)
        logits = jnp.dot(lhs_s[...], wr_ref[...], preferred_element_type=f32) + br_ref[...]
        _, comb = _route(logits, tg_ref[t].astype(f32))
        comb_s[...] = comb

    @pl.when(t < nt)
    def _():
        lane_e = ROUTE_EXPERT_LANE0 + EXPERTS_PER_GROUP * tg_ref[t] + e
        for r in range(tm // rows):
            rs = slice(r * rows, (r + 1) * rows)
            comb = comb_s[rs, :]
            lane = lax.broadcasted_iota(jnp.int32, comb.shape, 1)
            c = jnp.sum(jnp.where(lane == lane_e, comb, 0.0), axis=1, keepdims=True)
            up = jnp.dot(lhs_s[rs, :], wu_ref[...], preferred_element_type=f32)
            act = (jax.nn.silu(up[:, :ff]) * up[:, ff:]).astype(bf16)
            y = c * jnp.dot(act, wd_ref[...], preferred_element_type=f32)

            @pl.when(e == 0)
            def _():
                acc_s[rs, :] = y

            @pl.when(e > 0)
            def _():
                acc_s[rs, :] = acc_s[rs, :] + y

    @pl.when(jnp.logical_and(e == EXPERTS_PER_GROUP - 1, t < nt))
    def _():
        _slab_store(ys_ref, acc_s)

    @pl.when(jnp.logical_and(e == EXPERTS_PER_GROUP - 1, t >= nt))
    def _():
        ys_ref[...] = jnp.zeros_like(ys_ref)


def moe_sorted(slab, src, tile_group, nt, wr, br, wu, wd, layer, d):
    tm = MOE_TILE
    s_per = d // LANES
    nt_max = src.shape[0] // tm
    ff2 = wu.shape[3]
    ff = ff2 // 2
    rows = 256

    def widx(t, e, src_r, tg_r, nt_r):
        return tg_r[t] * EXPERTS_PER_GROUP + e

    grid_spec = pltpu.PrefetchScalarGridSpec(
        num_scalar_prefetch=3,
        grid=(nt_max, EXPERTS_PER_GROUP),
        in_specs=[pl.BlockSpec(memory_space=pl.ANY),
                  pl.BlockSpec((d, LANES), lambda t, e, *_: (0, 0)),
                  pl.BlockSpec((1, LANES), lambda t, e, *_: (0, 0)),
                  pl.BlockSpec((None, None, d, ff2), lambda t, e, *s: (layer, widx(t, e, *s), 0, 0)),
                  pl.BlockSpec((None, None, ff, d), lambda t, e, *s: (layer, widx(t, e, *s), 0, 0))],
        out_specs=pl.BlockSpec((tm * s_per, LANES), lambda t, e, *_: (t, 0)),
        scratch_shapes=[pltpu.VMEM((2, tm * s_per, LANES), f32),
                        pltpu.SemaphoreType.DMA((2,)),
                        pltpu.VMEM((tm, d), bf16),
                        pltpu.VMEM((tm, LANES), f32),
                        pltpu.VMEM((tm, d), f32)])
    return pl.pallas_call(
        functools.partial(_moe_kernel, tm=tm, s_per=s_per, ff=ff, rows=rows),
        grid_spec=grid_spec,
        out_shape=jax.ShapeDtypeStruct((nt_max * tm * s_per, LANES), f32),
        compiler_params=_params(("arbitrary", "arbitrary")),
        name="moe_sorted",
    )(src, tile_group, nt, slab, wr, br, wu, wd)


def _unperm_kernel(pos_ref, x1_ref, g_ref, ys_hbm, *rest, tm, s_per, final, tok0):
    if final:
        y_ref, gbuf, sem = rest
    else:
        x_ref, h_ref, gbuf, sem = rest
    m = pl.program_id(0)
    nm = pl.num_programs(0)

    def row_copy(row, i, slot):
        return pltpu.make_async_copy(ys_hbm.at[pl.ds(pl.multiple_of(row * s_per, s_per), s_per), :],
                                     gbuf.at[slot, pl.ds(pl.multiple_of(i * s_per, s_per), s_per), :],
                                     sem.at[slot])

    def issue(tile, slot):
        def body(i, _):
            row_copy(pos_ref[tok0 + tile * tm + i], i, slot).start()
            return 0
        lax.fori_loop(0, tm, body, 0)

    def wait_tile(slot):
        def body(i, _):
            row_copy(0, i, slot).wait()
            return 0
        lax.fori_loop(0, tm, body, 0)

    @pl.when(m == 0)
    def _():
        issue(0, 0)

    @pl.when(m + 1 < nm)
    def _():
        issue(m + 1, (m + 1) % 2)

    wait_tile(m % 2)
    x = x1_ref[...] + _slab_load(gbuf.at[m % 2], tm, s_per)
    if final:
        y_ref[...] = _rmsnorm_val(x, g_ref[...])
    else:
        x_ref[...] = x
        h_ref[...] = _rmsnorm_val(x, g_ref[...]).astype(h_ref.dtype)


def add_experts_norm(x1, ys, pos, g, *, final, row0=0, nrows=None):
    n, d = x1.shape
    nrows = n - row0 if nrows is None else nrows
    tm = _pick(nrows, (256, 128, 64, 32))
    assert row0 % tm == 0
    off = row0 // tm
    s_per = d // LANES
    row_in = pl.BlockSpec((tm, d), lambda m, *_: (m + off, 0))
    row_out = pl.BlockSpec((tm, d), lambda m, *_: (m, 0))
    if final:
        out_specs = row_out
        out_shape = jax.ShapeDtypeStruct((nrows, d), f32)
    else:
        out_specs = [row_out, row_out]
        out_shape = [jax.ShapeDtypeStruct((nrows, d), f32), jax.ShapeDtypeStruct((nrows, d), bf16)]
    grid_spec = pltpu.PrefetchScalarGridSpec(
        num_scalar_prefetch=1,
        grid=(nrows // tm,),
        in_specs=[row_in,
                  pl.BlockSpec((1, d), lambda m, *_: (0, 0)),
                  pl.BlockSpec(memory_space=pl.ANY)],
        out_specs=out_specs,
        scratch_shapes=[pltpu.VMEM((2, tm * s_per, LANES), f32),
                        pltpu.SemaphoreType.DMA((2,))])
    return pl.pallas_call(
        functools.partial(_unperm_kernel, tm=tm, s_per=s_per, final=final, tok0=row0),
        grid_spec=grid_spec,
        out_shape=out_shape,
        compiler_params=_params(("arbitrary",)),
        name="add_experts_norm",
    )(pos, x1, g.reshape(1, d), ys)


def kernel(x_prompt, x_sample, cache_k, cache_v, state_conv, state_lru, norm1_g, w_in, conv_w, conv_b,
           lru_wa, lru_ba, lru_wx, lru_bx, lru_lambda, w_pa, w_pb, w_o, norm2_g, router_group_w,
           router_group_b, router_sub_w, router_sub_b, expert_w_up, expert_w_down, final_norm_g):
    bp, tp, d = x_prompt.shape
    bs, ts, _ = x_sample.shape
    depth = w_in.shape[0]
    sbw = N_HEADS * HEAD_DIM
    lw = conv_w.shape[2]
    n_p, n_s = bp * tp, bs * ts

    x = jnp.concatenate([x_prompt.reshape(n_p, d), x_sample.reshape(n_s, d)], axis=0)
    zero_conv = jnp.zeros((bp, CONV_W - 1, lw), f32)
    zero_lru = jnp.zeros((bp, lw), f32)
    wu16 = expert_w_up.astype(bf16)
    wd16 = expert_w_down.astype(bf16)

    kp = vp = None
    ks_l, vs_l, cp_l, cs_l, hp_l, hs_l = [], [], [], [], [], []
    h1 = rmsnorm(x, norm1_g[0], bf16)
    for l in range(depth):
        q = matmul(h1, w_in, l, 0, sbw, bf16, scale=SB_SCALE)
        kp, kb = kv_matmul(h1, w_in, l, sbw, n_p, depth, kp)
        vp, vb = kv_matmul(h1, w_in, l, 2 * sbw, n_p, depth, vp)
        ks = matmul(h1, w_in, l, sbw, sbw, f32, row0=n_p, nrows=n_s)
        vs = matmul(h1, w_in, l, 2 * sbw, sbw, f32, row0=n_p, nrows=n_s)
        c0 = 3 * sbw
        xb = matmul(h1, w_in, l, c0, lw, f32)
        yg = matmul(h1, w_in, l, c0 + lw, lw, bf16)
        ga = matmul(h1, w_in, l, c0 + 2 * lw, d, bf16)
        gb = matmul(h1, w_in, l, c0 + 2 * lw + d, d, bf16)

        oa_p = sb_attention(q, kb, vb, batch=bp, seq=tp, row0=0)
        oa_s = sb_attention(q, ks, vs, batch=bs, seq=ts, row0=n_p, layer=l, past_k=cache_k, past_v=cache_v)

        lru_args = (conv_w[l], conv_b[l], lru_wa[l], lru_wx[l], lru_ba[l], lru_bx[l], lru_lambda[l])
        ob_p, cp, hp = rg_lru_mixer(xb, yg, zero_conv, zero_lru, *lru_args, batch=bp, seq=tp, row0=0)
        ob_s, cs, hs = rg_lru_mixer(xb, yg, state_conv[l], state_lru[l], *lru_args, batch=bs, seq=ts,
                                    row0=n_p)

        mixed = gated_mix(oa_p, oa_s, ob_p, ob_s, ga, gb, w_pa, w_pb, l)
        x1 = matmul(mixed, w_o, l, 0, d, f32, res=x)

        wr, br = _router_weights(router_group_w[l], router_group_b[l], router_sub_w[l], router_sub_b[l])
        slab, grp = router(x1, norm2_g[l], wr, br)
        src, pos, tile_group, nt = _group_layout(grp, MOE_TILE)
        ys = moe_sorted(slab, src, tile_group, nt, wr, br, wu16, wd16, l, d)
        if l + 1 < depth:
            x, h1 = add_experts_norm(x1, ys, pos, norm1_g[l + 1], final=False)
        else:
            y_prompt = add_experts_norm(x1, ys, pos, final_norm_g, final=True, row0=0, nrows=n_p)
            y_sample = add_experts_norm(x1, ys, pos, final_norm_g, final=True, row0=n_p, nrows=n_s)

        ks_l.append(ks)
        vs_l.append(vs)
        cp_l.append(cp)
        cs_l.append(cs)
        hp_l.append(hp)
        hs_l.append(hs)

    prompt_k = kp.reshape(depth, bp, tp, N_HEADS, HEAD_DIM)
    prompt_v = vp.reshape(depth, bp, tp, N_HEADS, HEAD_DIM)
    sample_k = jnp.stack(ks_l).reshape(depth, bs, ts, N_HEADS, HEAD_DIM)
    sample_v = jnp.stack(vs_l).reshape(depth, bs, ts, N_HEADS, HEAD_DIM)
    return (y_prompt.reshape(bp, tp, d), y_sample.reshape(bs, ts, d), prompt_k, prompt_v,
            jnp.stack(cp_l), jnp.stack(hp_l), sample_k, sample_v, jnp.stack(cs_l), jnp.stack(hs_l))
```

```python
import functools

import jax
import jax.numpy as jnp
from jax import lax
from jax.experimental import pallas as pl
from jax.experimental.pallas import tpu as pltpu

N_HEADS = 16
HEAD_DIM = 128
SB_SCALE = HEAD_DIM ** -0.5
LRU_BLOCKS = 16
CONV_W = 4
LRU_C = 8.0
N_GROUPS = 4
EXPERTS_PER_GROUP = 4
N_EXPERTS = N_GROUPS * EXPERTS_PER_GROUP
NORM_EPS = 1e-6

LANES = 128
SUBLANES = 8
VMEM_LIMIT = 56 * 1024 * 1024
SB_EXIT = -110.0
ATTN_HEADS = 8
ATTN_WINDOW = 3
ROUTE_EXPERT_LANE0 = 4
MOE_TILE = 512

f32 = jnp.float32
bf16 = jnp.bfloat16


def _pick(n, cands):
    for c in cands:
        if n % c == 0:
            return c
    raise ValueError(f"no tile size for {n}")


def _params(sem):
    return pltpu.CompilerParams(dimension_semantics=sem, vmem_limit_bytes=VMEM_LIMIT)


def _drop_ref(kern, pos, *refs):
    kern(*(refs[:pos] + refs[pos + 1:]))


def _aligned(x, m):
    return x if isinstance(x, int) else pl.multiple_of(x, m)


def _log_sigmoid(x):
    return jnp.minimum(x, 0.0) - jnp.log(1.0 + jnp.exp(-jnp.abs(x)))


def _slab_store(ref, val):
    tm, d = val.shape
    s_per = d // LANES
    for s in range(s_per):
        ref[pl.ds(s, tm, stride=s_per), :] = val[:, s * LANES:(s + 1) * LANES]


def _slab_load(ref, tm, s_per):
    return jnp.concatenate([ref[pl.ds(s, tm, stride=s_per), :] for s in range(s_per)], axis=1)


def _rmsnorm_val(x, g):
    ms = jnp.mean(x * x, axis=-1, keepdims=True)
    return (x * lax.rsqrt(ms + NORM_EPS)) * g


def _rmsnorm_kernel(x_ref, g_ref, o_ref):
    o_ref[...] = _rmsnorm_val(x_ref[...], g_ref[...]).astype(o_ref.dtype)


def rmsnorm(x, g, out_dtype):
    n, d = x.shape
    tm = _pick(n, (512, 256, 128, 64, 32, 16, 8))
    return pl.pallas_call(
        _rmsnorm_kernel,
        grid=(n // tm,),
        in_specs=[pl.BlockSpec((tm, d), lambda m: (m, 0)),
                  pl.BlockSpec((1, d), lambda m: (0, 0))],
        out_specs=pl.BlockSpec((tm, d), lambda m: (m, 0)),
        out_shape=jax.ShapeDtypeStruct((n, d), out_dtype),
        compiler_params=_params(("parallel",)),
        name="rmsnorm",
    )(x, g.reshape(1, d))


def _mm_kernel(*refs, scale, has_res):
    if has_res:
        h_ref, w_ref, r_ref, o_ref, wb_ref = refs
    else:
        h_ref, w_ref, o_ref, wb_ref = refs

    @pl.when(pl.program_id(1) == 0)
    def _():
        wb_ref[...] = w_ref[...].astype(bf16)

    acc = jnp.dot(h_ref[...], wb_ref[...], preferred_element_type=f32)
    if scale != 1.0:
        acc = acc * scale
    if has_res:
        acc = acc + r_ref[...]
    o_ref[...] = acc.astype(o_ref.dtype)


def matmul(h, w, layer, col0, ncols, out_dtype, *, scale=1.0, res=None, row0=0, nrows=None):
    n, kdim = h.shape
    nrows = n - row0 if nrows is None else nrows
    tm = _pick(nrows, (768, 512, 256, 128, 64, 32))
    tn = _pick(ncols, (1024, 512, 256, 128))
    assert row0 % tm == 0 and col0 % tn == 0
    roff, coff = row0 // tm, col0 // tn
    in_specs = [pl.BlockSpec((tm, kdim), lambda j, m: (m + roff, 0)),
                pl.BlockSpec((None, kdim, tn), lambda j, m: (layer, 0, j + coff))]
    args = [h, w]
    if res is not None:
        assert row0 == 0 and col0 == 0
        in_specs.append(pl.BlockSpec((tm, tn), lambda j, m: (m, j)))
        args.append(res)
    return pl.pallas_call(
        functools.partial(_mm_kernel, scale=scale, has_res=res is not None),
        grid=(ncols // tn, nrows // tm),
        in_specs=in_specs,
        out_specs=pl.BlockSpec((tm, tn), lambda j, m: (m, j)),
        out_shape=jax.ShapeDtypeStruct((nrows, ncols), out_dtype),
        scratch_shapes=[pltpu.VMEM((kdim, tn), bf16)],
        compiler_params=_params(("arbitrary", "arbitrary")),
        name="matmul",
    )(*args)


def _kv_kernel(h_ref, w_ref, o5_ref, ob_ref, wb_ref, *, layer, fill):
    @pl.when(jnp.logical_and(pl.program_id(1) == 0, pl.program_id(2) == 0))
    def _():
        wb_ref[...] = w_ref[...].astype(bf16)

    def project():
        pw = 2 * HEAD_DIM
        for c in range(o5_ref.shape[1] // 2):
            acc = jnp.dot(h_ref[...], wb_ref[:, c * pw:(c + 1) * pw], preferred_element_type=f32)
            ob_ref[:, c * pw:(c + 1) * pw] = acc.astype(bf16)
            o5_ref[:, 2 * c, :] = acc[:, :HEAD_DIM]
            o5_ref[:, 2 * c + 1, :] = acc[:, HEAD_DIM:]

    if fill:
        pl.when(pl.program_id(2) == layer)(project)

        @pl.when(pl.program_id(2) != layer)
        def _():
            o5_ref[...] = jnp.zeros_like(o5_ref)
    else:
        project()


def kv_matmul(h, w, layer, col0, nrows, depth, out_buf):
    n, kdim = h.shape
    ncols = N_HEADS * HEAD_DIM
    tm = _pick(nrows, (512, 256, 128, 64, 32))
    hpt = N_HEADS
    tn = hpt * HEAD_DIM
    coff = col0 // tn
    fill = out_buf is None
    in_specs = [pl.BlockSpec((tm, kdim), lambda j, m, s: (m, 0)),
                pl.BlockSpec((None, kdim, tn), lambda j, m, s: (layer, 0, j + coff),
                             pipeline_mode=pl.Buffered(1))]
    args = [h, w]
    kern = functools.partial(_kv_kernel, layer=layer, fill=fill)
    aliases = {}
    if fill:
        slab = lambda s: s
    else:
        slab = lambda s: layer
        in_specs.append(pl.BlockSpec(memory_space=pl.ANY))
        args.append(out_buf)
        aliases = {2: 0}
        kern = functools.partial(_drop_ref, kern, 2)
    return pl.pallas_call(
        kern,
        grid=(ncols // tn, nrows // tm, depth if fill else 1),
        in_specs=in_specs,
        out_specs=[pl.BlockSpec((None, tm, hpt, HEAD_DIM), lambda j, m, s: (slab(s), m, j, 0)),
                   pl.BlockSpec((tm, tn), lambda j, m, s: (m, j))],
        out_shape=[jax.ShapeDtypeStruct((depth, nrows, N_HEADS, HEAD_DIM), f32),
                   jax.ShapeDtypeStruct((nrows, ncols), bf16)],
        scratch_shapes=[pltpu.VMEM((kdim, tn), bf16)],
        input_output_aliases=aliases,
        compiler_params=_params(("arbitrary", "arbitrary", "arbitrary")),
        name="kv_matmul",
    )(*args)


def _pair_blockdiag(x2):
    lane = lax.broadcasted_iota(jnp.int32, x2.shape, 1)
    first = lane < HEAD_DIM
    zero = jnp.zeros_like(x2)
    return jnp.concatenate([jnp.where(first, x2, zero), jnp.where(first, zero, x2)], axis=0)


def _pair_cols(t0, t1, tq):
    return jnp.concatenate([jnp.broadcast_to(t0, (tq, LANES)), jnp.broadcast_to(t1, (tq, LANES))], axis=1)


def _sb_pair_block(q2, k2, v2, tri2, carry2, masked):
    tk = k2.shape[0]
    z = lax.dot_general(q2, _pair_blockdiag(k2), (((1,), (1,)), ((), ())), preferred_element_type=f32)
    lk = _log_sigmoid(-z)
    if masked:
        row = lax.broadcasted_iota(jnp.int32, z.shape, 0)
        col = lax.broadcasted_iota(jnp.int32, z.shape, 1)
        mask = jnp.where(col >= tk, col - tk, col) < row
        lkm = jnp.where(mask, lk, 0.0)
    else:
        lkm = lk
    hi = lkm.astype(bf16)
    lo = (lkm - hi.astype(f32)).astype(bf16)
    suf = (jnp.dot(hi, tri2, preferred_element_type=f32)
           + jnp.dot(lo, tri2, preferred_element_type=f32))
    arg = z + lk + suf
    if carry2 is not None:
        arg = arg + carry2
    w = jnp.exp(arg)
    if masked:
        w = jnp.where(mask, w, 0.0)
    out = jnp.dot(w.astype(bf16), _pair_blockdiag(v2), preferred_element_type=f32)
    t0 = jnp.sum(lkm[:, :tk], axis=1, keepdims=True)
    t1 = jnp.sum(lkm[:, tk:], axis=1, keepdims=True)
    return out, t0, t1


def _attn_kernel(*refs, tq, n_qblk, pairs, n_past_blk, layer):
    refs = list(refs)
    q_ref, k_ref, v_ref = refs[:3]
    pos = 3
    if n_past_blk:
        pk_hbm, pv_hbm = refs[3:5]
        pos = 5
    tri_ref = refs[pos]
    pos += 1
    trid_ref = tri_ref
    if tq != LANES:
        trid_ref = refs[pos]
        pos += 1
    o_ref = refs[pos]
    pos += 1
    acc = refs[pos:pos + pairs]
    car = refs[pos + pairs:pos + 2 * pairs]
    pos += 2 * pairs
    if n_past_blk:
        kbuf, vbuf, sem = refs[pos:pos + 3]
    pw = 2 * HEAD_DIM
    heads = 2 * pairs

    def ps(p):
        return slice(p * pw, (p + 1) * pw)

    def sweep(q2s, j0, load_kv, before=None):
        tri2 = tri_ref[...]

        def cond(st):
            j, mx = st
            return jnp.logical_and(j >= 0, mx > SB_EXIT)

        def body(st):
            j, _ = st
            if before is not None:
                before(j)
            mx = None
            for p in range(pairs):
                k2, v2 = load_kv(j, p)
                c2 = car[p][...]
                out, t0, t1 = _sb_pair_block(q2s[p], k2, v2, tri2, c2, False)
                acc[p][...] = acc[p][...] + out
                c2 = c2 + _pair_cols(t0, t1, tq)
                car[p][...] = c2
                m = jnp.max(c2)
                mx = m if mx is None else jnp.maximum(mx, m)
            return j - 1, mx

        mx0 = None
        for p in range(pairs):
            m = jnp.max(car[p][...])
            mx0 = m if mx0 is None else jnp.maximum(mx0, m)
        return lax.while_loop(cond, body, (j0, mx0))

    if n_past_blk:
        b = pl.program_id(0)
        h0 = pl.multiple_of(pl.program_id(1) * heads, heads)

        def cache_copies(j, slot):
            rows = pl.ds(pl.multiple_of(j * LANES, LANES), LANES)
            return (pltpu.make_async_copy(pk_hbm.at[layer, b, rows, pl.ds(h0, heads), :], kbuf.at[slot],
                                          sem.at[0, slot]),
                    pltpu.make_async_copy(pv_hbm.at[layer, b, rows, pl.ds(h0, heads), :], vbuf.at[slot],
                                          sem.at[1, slot]))

        def cache_start(j, slot):
            for c in cache_copies(j, slot):
                c.start()

        def cache_wait(j, slot):
            for c in cache_copies(j, slot):
                c.wait()

    def qblock(qb, _):
        r0 = pl.multiple_of(qb * tq, tq)
        if n_past_blk:
            cache_start(n_past_blk - 1, (n_past_blk - 1) % 2)
        q2s = [q_ref[pl.ds(r0, tq), ps(p)] for p in range(pairs)]
        trid = trid_ref[...]
        for p in range(pairs):
            kd = k_ref[pl.ds(r0, tq), ps(p)].astype(bf16)
            vd = v_ref[pl.ds(r0, tq), ps(p)].astype(bf16)
            out, t0, t1 = _sb_pair_block(q2s[p], kd, vd, trid, None, True)
            acc[p][...] = out
            car[p][...] = _pair_cols(t0, t1, tq)

        if n_qblk > 1:
            def load_own(j, p):
                rj = pl.multiple_of(j * tq, tq)
                return (k_ref[pl.ds(rj, tq), ps(p)].astype(bf16),
                        v_ref[pl.ds(rj, tq), ps(p)].astype(bf16))
            sweep(q2s, qb - 1, load_own)

        if n_past_blk:
            def before(j):
                cache_wait(j, j % 2)

                @pl.when(j >= 1)
                def _():
                    cache_start(j - 1, (j - 1) % 2)

            def load_past(j, p):
                kb = kbuf.at[j % 2]
                vb = vbuf.at[j % 2]
                k2 = jnp.concatenate([kb[:, 2 * p, :], kb[:, 2 * p + 1, :]], axis=1)
                v2 = jnp.concatenate([vb[:, 2 * p, :], vb[:, 2 * p + 1, :]], axis=1)
                return k2.astype(bf16), v2.astype(bf16)

            j_end, _ = sweep(q2s, n_past_blk - 1, load_past, before)

            @pl.when(j_end >= 0)
            def _():
                cache_wait(j_end, j_end % 2)

        for p in range(pairs):
            o_ref[pl.ds(r0, tq), ps(p)] = acc[p][...].astype(o_ref.dtype)
        return 0

    if n_qblk == 1:
        qblock(0, 0)
    else:
        lax.fori_loop(0, n_qblk, qblock, 0)


def _attn_wave_kernel(q_ref, k_ref, v_ref, tri_ref, o_ref, *scratch, n_qblk, pairs, win):
    acc = scratch[:pairs]
    car = scratch[pairs:2 * pairs]
    tq = LANES
    pw = 2 * HEAD_DIM

    def ps(p):
        return slice(p * pw, (p + 1) * pw)

    def step(j, nact):
        r0 = _aligned(j * tq, tq)
        tri2 = tri_ref[...]
        row = lax.broadcasted_iota(jnp.int32, (tq, 2 * tq), 0)
        col = lax.broadcasted_iota(jnp.int32, (tq, 2 * tq), 1)
        mask = jnp.where(col >= tq, col - tq, col) < row
        for p in range(pairs):
            q_all = q_ref[pl.ds(r0, nact * tq), ps(p)]
            kk = _pair_blockdiag(k_ref[pl.ds(r0, tq), ps(p)])
            vv = _pair_blockdiag(v_ref[pl.ds(r0, tq), ps(p)])
            z = lax.dot_general(q_all, kk, (((1,), (1,)), ((), ())), preferred_element_type=f32)
            his, los, args, tots = [], [], [], []
            for i in range(nact):
                zi = z[i * tq:(i + 1) * tq]
                lk = _log_sigmoid(-zi)
                lkm = jnp.where(mask, lk, 0.0) if i == 0 else lk
                hi = lkm.astype(bf16)
                his.append(hi)
                los.append((lkm - hi.astype(f32)).astype(bf16))
                args.append(zi + lk)
                tots.append(_pair_cols(jnp.sum(lkm[:, :tq], axis=1, keepdims=True),
                                       jnp.sum(lkm[:, tq:], axis=1, keepdims=True), tq))
            suf = (jnp.dot(jnp.concatenate(his, axis=0), tri2, preferred_element_type=f32)
                   + jnp.dot(jnp.concatenate(los, axis=0), tri2, preferred_element_type=f32))
            ws = []
            for i in range(nact):
                a = args[i] + suf[i * tq:(i + 1) * tq]
                if i == 0:
                    w = jnp.where(mask, jnp.exp(a), 0.0)
                else:
                    w = jnp.exp(a + car[p][(j + i) % win])
                ws.append(w.astype(bf16))
            out = jnp.dot(jnp.concatenate(ws, axis=0), vv, preferred_element_type=f32)
            for i in range(nact):
                slot = (j + i) % win
                if i == 0:
                    acc[p][slot] = out[:tq]
                    car[p][slot] = tots[0]
                else:
                    acc[p][slot] = acc[p][slot] + out[i * tq:(i + 1) * tq]
                    car[p][slot] = car[p][slot] + tots[i]

    def finalize(qb, j0):
        slot = qb % win
        rq = _aligned(qb * tq, tq)
        if not (isinstance(j0, int) and j0 < 0):
            tri2 = tri_ref[...]
            q2s = [q_ref[pl.ds(rq, tq), ps(p)] for p in range(pairs)]

            def cond(st):
                j, mx = st
                return jnp.logical_and(j >= 0, mx > SB_EXIT)

            def body(st):
                j, _ = st
                rj = pl.multiple_of(j * tq, tq)
                mx = None
                for p in range(pairs):
                    c2 = car[p][slot]
                    out, t0, t1 = _sb_pair_block(q2s[p], k_ref[pl.ds(rj, tq), ps(p)], v_ref[pl.ds(rj, tq), ps(p)],
                                                 tri2, c2, False)
                    acc[p][slot] = acc[p][slot] + out
                    c2 = c2 + _pair_cols(t0, t1, tq)
                    car[p][slot] = c2
                    m = jnp.max(c2)
                    mx = m if mx is None else jnp.maximum(mx, m)
                return j - 1, mx

            mx0 = None
            for p in range(pairs):
                m = jnp.max(car[p][slot])
                mx0 = m if mx0 is None else jnp.maximum(mx0, m)
            lax.while_loop(cond, body, (j0, mx0))
        for p in range(pairs):
            o_ref[pl.ds(rq, tq), ps(p)] = acc[p][slot].astype(o_ref.dtype)

    n_full = max(n_qblk - win + 1, 0)
    for j in range(n_qblk - 1, n_full - 1, -1):
        step(j, n_qblk - j)

    if n_full:
        def full_step(i, _):
            j = n_full - 1 - i
            step(j, win)
            finalize(j + win - 1, j - 1)
            return 0
        lax.fori_loop(0, n_full, full_step, 0)

    for qb in range(min(win - 1, n_qblk) - 1, -1, -1):
        finalize(qb, -1)


def _tri_pair(t):
    tri = (jnp.arange(t)[:, None] > jnp.arange(t)[None, :]).astype(bf16)
    z = jnp.zeros((t, t), bf16)
    return jnp.concatenate([jnp.concatenate([tri, z], axis=1), jnp.concatenate([z, tri], axis=1)], axis=0)


def sb_attention(q, k, v, *, batch, seq, row0, layer=0, past_k=None, past_v=None):
    n, width = q.shape
    tq = min(seq, LANES)
    n_qblk = seq // tq
    heads = ATTN_HEADS
    pairs = heads // 2
    gw = heads * HEAD_DIM
    assert seq % tq == 0 and row0 % seq == 0 and width % gw == 0
    boff = row0 // seq
    in_specs = [pl.BlockSpec((seq, gw), lambda b, h: (b + boff, h)),
                pl.BlockSpec((seq, gw), lambda b, h: (b, h)),
                pl.BlockSpec((seq, gw), lambda b, h: (b, h))]
    args = [q, k, v]
    scratch = ([pltpu.VMEM((tq, 2 * HEAD_DIM), f32) for _ in range(pairs)]
               + [pltpu.VMEM((tq, 2 * LANES), f32) for _ in range(pairs)])
    n_past_blk = 0
    if past_k is not None:
        past = past_k.shape[2]
        assert past % LANES == 0
        n_past_blk = past // LANES
        in_specs += [pl.BlockSpec(memory_space=pl.ANY), pl.BlockSpec(memory_space=pl.ANY)]
        args += [past_k, past_v]
        scratch += [pltpu.VMEM((2, LANES, heads, HEAD_DIM), f32),
                    pltpu.VMEM((2, LANES, heads, HEAD_DIM), f32),
                    pltpu.SemaphoreType.DMA((2, 2))]
    in_specs.append(pl.BlockSpec((2 * LANES, 2 * LANES), lambda b, h: (0, 0)))
    args.append(_tri_pair(LANES))
    if tq != LANES:
        in_specs.append(pl.BlockSpec((2 * tq, 2 * tq), lambda b, h: (0, 0)))
        args.append(_tri_pair(tq))
    if past_k is None and n_qblk > 1:
        win = ATTN_WINDOW
        scratch = ([pltpu.VMEM((win, tq, 2 * HEAD_DIM), f32) for _ in range(pairs)]
                   + [pltpu.VMEM((win, tq, 2 * LANES), f32) for _ in range(pairs)])
        kern = functools.partial(_attn_wave_kernel, n_qblk=n_qblk, pairs=pairs, win=win)
    else:
        kern = functools.partial(_attn_kernel, tq=tq, n_qblk=n_qblk, pairs=pairs, n_past_blk=n_past_blk,
                                 layer=layer)
    return pl.pallas_call(
        kern,
        grid=(batch, width // gw),
        in_specs=in_specs,
        out_specs=pl.BlockSpec((seq, gw), lambda b, h: (b, h)),
        out_shape=jax.ShapeDtypeStruct((batch * seq, width), bf16),
        scratch_shapes=scratch,
        compiler_params=_params(("parallel", "parallel")),
        name="sb_attention",
    )(*args)


def _one_minus_sq(log_a, a):
    y = 2.0 * log_a
    series = -y * (1.0 + y * (1.0 / 2 + y * (1.0 / 6)))
    return jnp.where(y > -0.01, series, 1.0 - a * a)


def _lru_kernel(xb_ref, yg_ref, cprev_ref, h0_ref, cw_ref, cb_ref, wa_ref, wx_ref, ba_ref, bx_ref,
                lam_ref, ob_ref, cnew_ref, hlast_ref, xpad_s, a_s, u_s, hs_s, h_s, *, tt, n_chunks):
    tc = pl.program_id(1)
    npad = SUBLANES
    nprev = CONV_W - 1

    @pl.when(tc == 0)
    def _():
        xpad_s[npad - nprev:npad, :] = cprev_ref[...]
        h_s[...] = h0_ref[...]

    @pl.when(tc > 0)
    def _():
        xpad_s[npad - nprev:npad, :] = xpad_s[npad + tt - nprev:npad + tt, :]

    xpad_s[npad:npad + tt, :] = xb_ref[...]
    xc = jnp.broadcast_to(cb_ref[...], (tt, cb_ref.shape[1]))
    for tap in range(CONV_W):
        s = npad - nprev + tap
        xc = xc + xpad_s[s:s + tt, :] * cw_ref[tap:tap + 1, :]

    log_sig_lam = _log_sigmoid(lam_ref[...])
    bw = LANES
    for nb in range(LRU_BLOCKS):
        cs = slice(nb * bw, (nb + 1) * bw)
        xcb = xc[:, cs]
        xcb16 = xcb.astype(bf16)
        r = jax.nn.sigmoid(jnp.dot(xcb16, wa_ref[nb], preferred_element_type=f32) + ba_ref[:, cs])
        i = jax.nn.sigmoid(jnp.dot(xcb16, wx_ref[nb], preferred_element_type=f32) + bx_ref[:, cs])
        log_a = (LRU_C * r) * log_sig_lam[:, cs]
        a = jnp.exp(log_a)
        a_s[:, cs] = a
        u_s[:, cs] = jnp.sqrt(_one_minus_sq(log_a, a)) * (i * xcb)

    def step(t, h):
        h = a_s[pl.ds(t, 1), :] * h + u_s[pl.ds(t, 1), :]
        hs_s[pl.ds(t, 1), :] = h
        return h

    h = lax.fori_loop(0, tt, step, h_s[...], unroll=8)
    h_s[...] = h
    ob_ref[...] = (jax.nn.gelu(yg_ref[...].astype(f32)) * hs_s[...]).astype(ob_ref.dtype)

    @pl.when(tc == n_chunks - 1)
    def _():
        cnew_ref[...] = xpad_s[npad + tt - nprev:npad + tt, :]
        hlast_ref[...] = h


def rg_lru_mixer(xb, yg, conv_prev, h0, conv_w, conv_b, wa, wx, ba, bx, lam, *, batch, seq, row0):
    n, width = xb.shape
    tt = min(seq, 256)
    n_chunks = seq // tt
    assert seq % tt == 0 and row0 % tt == 0
    coff = row0 // tt
    row_spec = pl.BlockSpec((tt, width), lambda b, c: (coff + b * n_chunks + c, 0))
    out_row_spec = pl.BlockSpec((tt, width), lambda b, c: (b * n_chunks + c, 0))
    vec = lambda r: pl.BlockSpec((r, width), lambda b, c: (0, 0))
    blk = pl.BlockSpec((LRU_BLOCKS, LANES, LANES), lambda b, c: (0, 0, 0))
    in_specs = [row_spec, row_spec,
                pl.BlockSpec((None, CONV_W - 1, width), lambda b, c: (b, 0, 0)),
                pl.BlockSpec((None, 1, width), lambda b, c: (b, 0, 0)),
                vec(CONV_W), vec(1), blk, blk, vec(1), vec(1), vec(1)]
    args = [xb, yg, conv_prev, h0.reshape(batch, 1, width), conv_w, conv_b.reshape(1, width),
            wa.astype(bf16), wx.astype(bf16), ba.reshape(1, width), bx.reshape(1, width),
            lam.reshape(1, width)]
    ob, cnew, hlast = pl.pallas_call(
        functools.partial(_lru_kernel, tt=tt, n_chunks=n_chunks),
        grid=(batch, n_chunks),
        in_specs=in_specs,
        out_specs=[out_row_spec,
                   pl.BlockSpec((None, CONV_W - 1, width), lambda b, c: (b, 0, 0)),
                   pl.BlockSpec((None, 1, width), lambda b, c: (b, 0, 0))],
        out_shape=[jax.ShapeDtypeStruct((batch * seq, width), bf16),
                   jax.ShapeDtypeStruct((batch, CONV_W - 1, width), f32),
                   jax.ShapeDtypeStruct((batch, 1, width), f32)],
        scratch_shapes=[pltpu.VMEM((SUBLANES + tt, width), f32),
                        pltpu.VMEM((tt, width), f32),
                        pltpu.VMEM((tt, width), f32),
                        pltpu.VMEM((tt, width), f32),
                        pltpu.VMEM((1, width), f32)],
        compiler_params=_params(("parallel", "arbitrary")),
        name="rg_lru_mixer",
    )(*args)
    return ob, cnew, hlast.reshape(batch, width)


def _mix_kernel(oap_ref, oas_ref, obp_ref, obs_ref, ga_ref, gb_ref, wpa_ref, wpb_ref, o_ref, wa_s, wb_s,
                *, prompt_tiles):
    m = pl.program_id(1)

    @pl.when(m == 0)
    def _():
        wa_s[...] = wpa_ref[...].astype(bf16)
        wb_s[...] = wpb_ref[...].astype(bf16)

    def mix(oa_ref, ob_ref):
        pa = jnp.dot(oa_ref[...], wa_s[...], preferred_element_type=f32)
        pb = jnp.dot(ob_ref[...], wb_s[...], preferred_element_type=f32)
        mixed = (jax.nn.sigmoid(ga_ref[...].astype(f32)) * pa
                 + jax.nn.sigmoid(gb_ref[...].astype(f32)) * pb)
        o_ref[...] = mixed.astype(o_ref.dtype)

    pl.when(m < prompt_tiles)(functools.partial(mix, oap_ref, obp_ref))
    pl.when(m >= prompt_tiles)(functools.partial(mix, oas_ref, obs_ref))


def gated_mix(oa_p, oa_s, ob_p, ob_s, ga, gb, w_pa, w_pb, layer):
    n_p, kdim = oa_p.shape
    n_s = oa_s.shape[0]
    ncols = w_pa.shape[2]
    tm = _pick(n_s, (256, 128, 64, 32))
    assert n_p % tm == 0
    pt = n_p // tm
    tn = _pick(ncols, (1024, 512, 256, 128))
    act_p =pl.BlockSpec((tm, kdim), lambda j, m: (jnp.minimum(m, pt - 1), 0))
    act_s = pl.BlockSpec((tm, kdim), lambda j, m: (jnp.maximum(m - pt, 0), 0))
    gate = pl.BlockSpec((tm, tn), lambda j, m: (m, j))
    wsp = pl.BlockSpec((None, kdim, tn), lambda j, m: (layer, 0, j))
    return pl.pallas_call(
        functools.partial(_mix_kernel, prompt_tiles=pt),
        grid=(ncols // tn, (n_p + n_s) // tm),
        in_specs=[act_p, act_s, act_p, act_s, gate, gate, wsp, wsp],
        out_specs=gate,
        out_shape=jax.ShapeDtypeStruct((n_p + n_s, ncols), bf16),
        scratch_shapes=[pltpu.VMEM((kdim, tn), bf16), pltpu.VMEM((kdim, tn), bf16)],
        compiler_params=_params(("arbitrary", "arbitrary")),
        name="gated_mix",
    )(oa_p, oa_s, ob_p, ob_s, ga, gb, w_pa, w_pb)


def _route(logits, forced_group=None):
    lane = lax.broadcasted_iota(jnp.int32, logits.shape, 1).astype(f32)
    neg = jnp.float32(-3.0e38)
    big = jnp.float32(LANES)

    def first_argmax(vals):
        m = jnp.max(vals, axis=1, keepdims=True)
        idx = jnp.min(jnp.where(vals == m, lane, big), axis=1, keepdims=True)
        return m, idx

    group_lanes = lane < N_GROUPS
    gl = jnp.where(group_lanes, logits, neg)
    gmax, gidx = first_argmax(gl)
    if forced_group is None:
        g_logit = gmax
    else:
        gidx = jnp.full_like(gidx, forced_group)
        g_logit = jnp.sum(jnp.where(lane == gidx, logits, 0.0), axis=1, keepdims=True)
    p_g = jnp.exp(g_logit - gmax) / jnp.sum(jnp.where(group_lanes, jnp.exp(gl - gmax), 0.0), axis=1,
                                             keepdims=True)
    lo = ROUTE_EXPERT_LANE0 + EXPERTS_PER_GROUP * gidx
    sel = jnp.logical_and(lane >= lo, lane < lo + EXPERTS_PER_GROUP)
    sl = jnp.where(sel, logits, neg)
    m1, i1 = first_argmax(sl)
    sl2 = jnp.where(lane == i1, neg, sl)
    m2, i2 = first_argmax(sl2)
    e2 = jnp.exp(m2 - m1)
    den = 1.0 + e2
    w1 = (1.0 / den) * p_g
    w2 = (e2 / den) * p_g
    comb = jnp.where(lane == i1, w1, 0.0) + jnp.where(lane == i2, w2, 0.0)
    return gidx, comb


def _router_weights(wg, bg, ws, bs):
    d = wg.shape[0]
    pad = LANES - N_GROUPS - N_EXPERTS
    wsub = jnp.transpose(ws, (1, 0, 2)).reshape(d, N_EXPERTS)
    wr = jnp.concatenate([wg, wsub, jnp.zeros((d, pad), f32)], axis=1).astype(bf16)
    br = jnp.concatenate([bg, bs.reshape(-1), jnp.zeros((pad,), f32)]).reshape(1, LANES)
    return wr, br


def _router_kernel(x_ref, g_ref, wr_ref, br_ref, slab_ref, grp_ref):
    h = _rmsnorm_val(x_ref[...], g_ref[...]).astype(bf16)
    logits = jnp.dot(h, wr_ref[...], preferred_element_type=f32) + br_ref[...]
    gidx, _ = _route(logits)
    grp_ref[...] = jnp.broadcast_to(gidx, grp_ref.shape)
    _slab_store(slab_ref, h.astype(f32))


def router(x, g, wr, br):
    n, d = x.shape
    tm = _pick(n, (256, 128, 64, 32))
    s_per = d // LANES
    return pl.pallas_call(
        _router_kernel,
        grid=(n // tm,),
        in_specs=[pl.BlockSpec((tm, d), lambda m: (m, 0)),
                  pl.BlockSpec((1, d), lambda m: (0, 0)),
                  pl.BlockSpec((d, LANES), lambda m: (0, 0)),
                  pl.BlockSpec((1, LANES), lambda m: (0, 0))],
        out_specs=[pl.BlockSpec((tm * s_per, LANES), lambda m: (m, 0)),
                   pl.BlockSpec((tm, LANES), lambda m: (m, 0))],
        out_shape=[jax.ShapeDtypeStruct((n * s_per, LANES), f32),
                   jax.ShapeDtypeStruct((n, LANES), f32)],
        compiler_params=_params(("parallel",)),
        name="router",
    )(x, g.reshape(1, d), wr, br)


def _group_layout(grp, tm):
    n = grp.shape[0]
    g = grp[:, 0].astype(jnp.int32)
    onehot = (g[:, None] == jnp.arange(N_GROUPS, dtype=jnp.int32)[None, :]).astype(jnp.int32)
    csum = jnp.cumsum(onehot, axis=0)
    counts = csum[-1]
    rank = jnp.sum(onehot * csum, axis=1) - 1
    tiles = (counts + tm - 1) // tm
    tile_end = jnp.cumsum(tiles)
    tile_start = tile_end - tiles
    pos = (jnp.sum(onehot * tile_start[None, :], axis=1) * tm + rank).astype(jnp.int32)
    nt_max = (n + N_GROUPS * (tm - 1)) // tm
    src = jnp.zeros((nt_max * tm,), jnp.int32).at[pos].set(jnp.arange(n, dtype=jnp.int32))
    tix = jnp.arange(nt_max, dtype=jnp.int32)
    tile_group = jnp.minimum(jnp.sum((tix[:, None] >= tile_end[None, :]).astype(jnp.int32), axis=1),
                             N_GROUPS - 1).astype(jnp.int32)
    return src, pos, tile_group, tile_end[-1:].astype(jnp.int32)


def _moe_kernel(src_ref, tg_ref, nt_ref, slab_hbm, wr_ref, br_ref, wu_ref, wd_ref, ys_ref,
                gbuf, sem, lhs_s, comb_s, acc_s, *, tm, s_per, ff, rows):
    t = pl.program_id(0)
    e = pl.program_id(1)
    nt = nt_ref[0]

    def row_copy(row, i, slot):
        return pltpu.make_async_copy(slab_hbm.at[pl.ds(pl.multiple_of(row * s_per, s_per), s_per), :],
                                     gbuf.at[slot, pl.ds(pl.multiple_of(i * s_per, s_per), s_per), :],
                                     sem.at[slot])

    def issue(tile, slot):
        def body(i, _):
            row_copy(src_ref[tile * tm + i], i, slot).start()
            return 0
        lax.fori_loop(0, tm, body, 0, unroll=8)

    def wait_tile(slot):
        pltpu.make_async_copy(slab_hbm.at[pl.ds(0, tm * s_per), :], gbuf.at[slot], sem.at[slot]).wait()

    @pl.when(jnp.logical_and(e == 0, t < nt))
    def _():
        @pl.when(t == 0)
        def _():
            issue(0, 0)

        @pl.when(t + 1 < nt)
        def _():
            issue(t + 1, (t + 1) % 2)

        wait_tile(t % 2)
        tile = gbuf.at[t % 2]
        for s in range(s_per):
            lhs_s[:, s * LANES:(s + 1) * LANES] = tile[pl.ds(s, tm, stride=s_per), :].astype(bf16)
        logits = jnp.dot(lhs_s[...], wr_ref[...], preferred_element_type=f32) + br_ref[...]
        _, comb = _route(logits, tg_ref[t].astype(f32))
        comb_s[...] = comb

    @pl.when(t < nt)
    def _():
        lane_e = ROUTE_EXPERT_LANE0 + EXPERTS_PER_GROUP * tg_ref[t] + e
        for r in range(tm // rows):
            rs = slice(r * rows, (r + 1) * rows)
            comb = comb_s[rs, :]
            lane = lax.broadcasted_iota(jnp.int32, comb.shape, 1)
            c = jnp.sum(jnp.where(lane == lane_e, comb, 0.0), axis=1, keepdims=True)
            up = jnp.dot(lhs_s[rs, :], wu_ref[...], preferred_element_type=f32)
            act = (jax.nn.silu(up[:, :ff]) * up[:, ff:]).astype(bf16)
            y = c * jnp.dot(act, wd_ref[...], preferred_element_type=f32)

            @pl.when(e == 0)
            def _():
                acc_s[rs, :] = y

            @pl.when(e > 0)
            def _():
                acc_s[rs, :] = acc_s[rs, :] + y

    @pl.when(jnp.logical_and(e == EXPERTS_PER_GROUP - 1, t < nt))
    def _():
        _slab_store(ys_ref, acc_s)

    @pl.when(jnp.logical_and(e == EXPERTS_PER_GROUP - 1, t >= nt))
    def _():
        ys_ref[...] = jnp.zeros_like(ys_ref)


def moe_sorted(slab, src, tile_group, nt, wr, br, wu, wd, layer, d):
    tm = MOE_TILE
    s_per = d // LANES
    nt_max = src.shape[0] // tm
    ff2 = wu.shape[3]
    ff = ff2 // 2
    rows = 256

    def widx(t, e, src_r, tg_r, nt_r):
        return tg_r[t] * EXPERTS_PER_GROUP + e

    grid_spec = pltpu.PrefetchScalarGridSpec(
        num_scalar_prefetch=3,
        grid=(nt_max, EXPERTS_PER_GROUP),
        in_specs=[pl.BlockSpec(memory_space=pl.ANY),
                  pl.BlockSpec((d, LANES), lambda t, e, *_: (0, 0)),
                  pl.BlockSpec((1, LANES), lambda t, e, *_: (0, 0)),
                  pl.BlockSpec((None, None, d, ff2), lambda t, e, *s: (layer, widx(t, e, *s), 0, 0)),
                  pl.BlockSpec((None, None, ff, d), lambda t, e, *s: (layer, widx(t, e, *s), 0, 0))],
        out_specs=pl.BlockSpec((tm * s_per, LANES), lambda t, e, *_: (t, 0)),
        scratch_shapes=[pltpu.VMEM((2, tm * s_per, LANES), f32),
                        pltpu.SemaphoreType.DMA((2,)),
                        pltpu.VMEM((tm, d), bf16),
                        pltpu.VMEM((tm, LANES), f32),
                        pltpu.VMEM((tm, d), f32)])
    return pl.pallas_call(
        functools.partial(_moe_kernel, tm=tm, s_per=s_per, ff=ff, rows=rows),
        grid_spec=grid_spec,
        out_shape=jax.ShapeDtypeStruct((nt_max * tm * s_per, LANES), f32),
        compiler_params=_params(("arbitrary", "arbitrary")),
        name="moe_sorted",
    )(src, tile_group, nt, slab, wr, br, wu, wd)


def _unperm_kernel(pos_ref, x1_ref, g_ref, ys_hbm, *rest, tm, s_per, final, tok0):
    if final:
        y_ref, gbuf, sem = rest
    else:
        x_ref, h_ref, gbuf, sem = rest
    m = pl.program_id(0)
    nm = pl.num_programs(0)

    def row_copy(row, i, slot):
        return pltpu.make_async_copy(ys_hbm.at[pl.ds(pl.multiple_of(row * s_per, s_per), s_per), :],
                                     gbuf.at[slot, pl.ds(pl.multiple_of(i * s_per, s_per), s_per), :],
                                     sem.at[slot])

    def issue(tile, slot):
        def body(i, _):
            row_copy(pos_ref[tok0 + tile * tm + i], i, slot).start()
            return 0
        lax.fori_loop(0, tm, body, 0, unroll=8)

    def wait_tile(slot):
        pltpu.make_async_copy(ys_hbm.at[pl.ds(0, tm * s_per), :], gbuf.at[slot], sem.at[slot]).wait()

    @pl.when(m == 0)
    def _():
        issue(0, 0)

    @pl.when(m + 1 < nm)
    def _():
        issue(m + 1, (m + 1) % 2)

    wait_tile(m % 2)
    x = x1_ref[...] + _slab_load(gbuf.at[m % 2], tm, s_per)
    if final:
        y_ref[...] = _rmsnorm_val(x, g_ref[...])
    else:
        x_ref[...] = x
        h_ref[...] = _rmsnorm_val(x, g_ref[...]).astype(h_ref.dtype)


def add_experts_norm(x1, ys, pos, g, *, final, row0=0, nrows=None):
    n, d = x1.shape
    nrows = n - row0 if nrows is None else nrows
    tm = _pick(nrows, (256, 128, 64, 32))
    assert row0 % tm == 0
    off = row0 // tm
    s_per = d // LANES
    row_in = pl.BlockSpec((tm, d), lambda m, *_: (m + off, 0))
    row_out = pl.BlockSpec((tm, d), lambda m, *_: (m, 0))
    if final:
        out_specs = row_out
        out_shape = jax.ShapeDtypeStruct((nrows, d), f32)
    else:
        out_specs = [row_out, row_out]
        out_shape = [jax.ShapeDtypeStruct((nrows, d), f32), jax.ShapeDtypeStruct((nrows, d), bf16)]
    grid_spec = pltpu.PrefetchScalarGridSpec(
        num_scalar_prefetch=1,
        grid=(nrows // tm,),
        in_specs=[row_in,
                  pl.BlockSpec((1, d), lambda m, *_: (0, 0)),
                  pl.BlockSpec(memory_space=pl.ANY)],
        out_specs=out_specs,
        scratch_shapes=[pltpu.VMEM((2, tm * s_per, LANES), f32),
                        pltpu.SemaphoreType.DMA((2,))])
    return pl.pallas_call(
        functools.partial(_unperm_kernel, tm=tm, s_per=s_per, final=final, tok0=row0),
        grid_spec=grid_spec,
        out_shape=out_shape,
        compiler_params=_params(("arbitrary",)),
        name="add_experts_norm",
    )(pos, x1, g.reshape(1, d), ys)


def kernel(x_prompt, x_sample, cache_k, cache_v, state_conv, state_lru, norm1_g, w_in, conv_w, conv_b,
           lru_wa, lru_ba, lru_wx, lru_bx, lru_lambda, w_pa, w_pb, w_o, norm2_g, router_group_w,
           router_group_b, router_sub_w, router_sub_b, expert_w_up, expert_w_down, final_norm_g):
    bp, tp, d = x_prompt.shape
    bs, ts, _ = x_sample.shape
    depth = w_in.shape[0]
    sbw = N_HEADS * HEAD_DIM
    lw = conv_w.shape[2]
    n_p, n_s = bp * tp, bs * ts

    x = jnp.concatenate([x_prompt.reshape(n_p, d), x_sample.reshape(n_s, d)], axis=0)
    zero_conv = jnp.zeros((bp, CONV_W - 1, lw), f32)
    zero_lru = jnp.zeros((bp, lw), f32)
    wu16 = expert_w_up.astype(bf16)
    wd16 = expert_w_down.astype(bf16)

    kp = vp = None
    ks_l, vs_l, cp_l, cs_l, hp_l, hs_l = [], [], [], [], [], []
    h1 = rmsnorm(x, norm1_g[0], bf16)
    for l in range(depth):
        q = matmul(h1, w_in, l, 0, sbw, bf16, scale=SB_SCALE)
        kp, kb = kv_matmul(h1, w_in, l, sbw, n_p, depth, kp)
        vp, vb = kv_matmul(h1, w_in, l, 2 * sbw, n_p, depth, vp)
        ks = matmul(h1, w_in, l, sbw, sbw, f32, row0=n_p, nrows=n_s)
        vs = matmul(h1, w_in, l, 2 * sbw, sbw, f32, row0=n_p, nrows=n_s)
        c0 = 3 * sbw
        xb = matmul(h1, w_in, l, c0, lw, f32)
        yg = matmul(h1, w_in, l, c0 + lw, lw, bf16)
        ga = matmul(h1, w_in, l, c0 + 2 * lw, d, bf16)
        gb = matmul(h1, w_in, l, c0 + 2 * lw + d, d, bf16)

        oa_p = sb_attention(q, kb, vb, batch=bp, seq=tp, row0=0)
        oa_s = sb_attention(q, ks, vs, batch=bs, seq=ts, row0=n_p, layer=l, past_k=cache_k, past_v=cache_v)

        lru_args = (conv_w[l], conv_b[l], lru_wa[l], lru_wx[l], lru_ba[l], lru_bx[l], lru_lambda[l])
        ob_p, cp, hp = rg_lru_mixer(xb, yg, zero_conv, zero_lru, *lru_args, batch=bp, seq=tp, row0=0)
        ob_s, cs, hs = rg_lru_mixer(xb, yg, state_conv[l], state_lru[l], *lru_args, batch=bs, seq=ts,
                                    row0=n_p)

        mixed = gated_mix(oa_p, oa_s, ob_p, ob_s, ga, gb, w_pa, w_pb, l)
        x1 = matmul(mixed, w_o, l, 0, d, f32, res=x)

        wr, br = _router_weights(router_group_w[l], router_group_b[l], router_sub_w[l], router_sub_b[l])
        slab, grp = router(x1, norm2_g[l], wr, br)
        src, pos, tile_group, nt = _group_layout(grp, MOE_TILE)
        ys = moe_sorted(slab, src, tile_group, nt, wr, br, wu16, wd16, l, d)
        if l + 1 < depth:
            x, h1 = add_experts_norm(x1, ys, pos, norm1_g[l + 1], final=False)
        else:
            y_prompt = add_experts_norm(x1, ys, pos, final_norm_g, final=True, row0=0, nrows=n_p)
            y_sample = add_experts_norm(x1, ys, pos, final_norm_g, final=True, row0=n_p, nrows=n_s)

        ks_l.append(ks)
        vs_l.append(vs)
        cp_l.append(cp)
        cs_l.append(cs)
        hp_l.append(hp)
        hs_l.append(hs)

    prompt_k = kp.reshape(depth, bp, tp, N_HEADS, HEAD_DIM)
    prompt_v = vp.reshape(depth, bp, tp, N_HEADS, HEAD_DIM)
    sample_k = jnp.stack(ks_l).reshape(depth, bs, ts, N_HEADS, HEAD_DIM)
    sample_v = jnp.stack(vs_l).reshape(depth, bs, ts, N_HEADS, HEAD_DIM)
    return (y_prompt.reshape(bp, tp, d), y_sample.reshape(bs, ts, d), prompt_k, prompt_v,
            jnp.stack(cp_l), jnp.stack(hp_l), sample_k, sample_v, jnp.stack(cs_l), jnp.stack(hs_l))
```

```python
import functools

import jax
import jax.numpy as jnp
from jax import lax
from jax.experimental import pallas as pl
from jax.experimental.pallas import tpu as pltpu

N_HEADS = 16
HEAD_DIM = 128
SB_SCALE = HEAD_DIM ** -0.5
LRU_BLOCKS = 16
CONV_W = 4
LRU_C = 8.0
N_GROUPS = 4
EXPERTS_PER_GROUP = 4
N_EXPERTS = N_GROUPS * EXPERTS_PER_GROUP
NORM_EPS = 1e-6

LANES = 128
SUBLANES = 8
VMEM_LIMIT = 56 * 1024 * 1024
SB_EXIT = -110.0
ATTN_HEADS = 8
ATTN_WINDOW = 3
ROUTE_EXPERT_LANE0 = 4
MOE_TILE = 768

f32 = jnp.float32
bf16 = jnp.bfloat16


def _pick(n, cands):
    for c in cands:
        if n % c == 0:
            return c
    raise ValueError(f"no tile size for {n}")


def _params(sem):
    return pltpu.CompilerParams(dimension_semantics=sem, vmem_limit_bytes=VMEM_LIMIT)


def _drop_ref(kern, pos, *refs):
    kern(*(refs[:pos] + refs[pos + 1:]))


def _aligned(x, m):
    return x if isinstance(x, int) else pl.multiple_of(x, m)


def _log_sigmoid(x):
    return jnp.minimum(x, 0.0) - jnp.log(1.0 + jnp.exp(-jnp.abs(x)))


def _slab_store(ref, val):
    tm, d = val.shape
    s_per = d // LANES
    for s in range(s_per):
        ref[pl.ds(s, tm, stride=s_per), :] = val[:, s * LANES:(s + 1) * LANES]


def _slab_load(ref, tm, s_per):
    return jnp.concatenate([ref[pl.ds(s, tm, stride=s_per), :] for s in range(s_per)], axis=1)


def _rmsnorm_val(x, g):
    ms = jnp.mean(x * x, axis=-1, keepdims=True)
    return (x * lax.rsqrt(ms + NORM_EPS)) * g


def _join_norm_kernel(xp_ref, xs_ref, g_ref, x_ref, h_ref, *, prompt_tiles):
    def emit(src_ref):
        x = src_ref[...]
        x_ref[...] = x
        h_ref[...] = _rmsnorm_val(x, g_ref[...]).astype(h_ref.dtype)

    pl.when(pl.program_id(0) < prompt_tiles)(functools.partial(emit, xp_ref))
    pl.when(pl.program_id(0) >= prompt_tiles)(functools.partial(emit, xs_ref))


def join_norm(x_p, x_s, g):
    n_p, d = x_p.shape
    n_s = x_s.shape[0]
    tm = _pick(n_s, (256, 128, 64, 32, 16, 8))
    assert n_p % tm == 0
    pt = n_p // tm
    row_out = pl.BlockSpec((tm, d), lambda m: (m, 0))
    return pl.pallas_call(
        functools.partial(_join_norm_kernel, prompt_tiles=pt),
        grid=((n_p + n_s) // tm,),
        in_specs=[pl.BlockSpec((tm, d), lambda m: (jnp.minimum(m, pt - 1), 0)),
                  pl.BlockSpec((tm, d), lambda m: (jnp.maximum(m - pt, 0), 0)),
                  pl.BlockSpec((1, d), lambda m: (0, 0))],
        out_specs=[row_out, row_out],
        out_shape=[jax.ShapeDtypeStruct((n_p + n_s, d), f32),
                   jax.ShapeDtypeStruct((n_p + n_s, d), bf16)],
        compiler_params=_params(("parallel",)),
        name="join_norm",
    )(x_p, x_s, g.reshape(1, d))


def _mm_kernel(*refs, scale, has_res):
    if has_res:
        h_ref, w_ref, r_ref, o_ref, wb_ref = refs
    else:
        h_ref, w_ref, o_ref, wb_ref = refs

    @pl.when(pl.program_id(1) == 0)
    def _():
        wb_ref[...] = w_ref[...].astype(bf16)

    acc = jnp.dot(h_ref[...], wb_ref[...], preferred_element_type=f32)
    if scale != 1.0:
        acc = acc * scale
    if has_res:
        acc = acc + r_ref[...]
    o_ref[...] = acc.astype(o_ref.dtype)


def matmul(h, w, layer, col0, ncols, out_dtype, *, scale=1.0, res=None, row0=0, nrows=None):
    n, kdim = h.shape
    nrows = n - row0 if nrows is None else nrows
    tm = _pick(nrows, (768, 512, 256, 128, 64, 32))
    tn = _pick(ncols, (1024, 512, 256, 128))
    assert row0 % tm == 0 and col0 % tn == 0
    roff, coff = row0 // tm, col0 // tn
    in_specs = [pl.BlockSpec((tm, kdim), lambda j, m: (m + roff, 0)),
                pl.BlockSpec((None, kdim, tn), lambda j, m: (layer, 0, j + coff))]
    args = [h, w]
    if res is not None:
        assert row0 == 0 and col0 == 0
        in_specs.append(pl.BlockSpec((tm, tn), lambda j, m: (m, j)))
        args.append(res)
    return pl.pallas_call(
        functools.partial(_mm_kernel, scale=scale, has_res=res is not None),
        grid=(ncols // tn, nrows // tm),
        in_specs=in_specs,
        out_specs=pl.BlockSpec((tm, tn), lambda j, m: (m, j)),
        out_shape=jax.ShapeDtypeStruct((nrows, ncols), out_dtype),
        scratch_shapes=[pltpu.VMEM((kdim, tn), bf16)],
        compiler_params=_params(("arbitrary", "arbitrary")),
        name="matmul",
    )(*args)


def _kv_kernel(h_ref, w_ref, o5_ref, ob_ref, wb_ref, *, layer, fill):
    @pl.when(jnp.logical_and(pl.program_id(1) == 0, pl.program_id(2) == 0))
    def _():
        wb_ref[...] = w_ref[...].astype(bf16)

    def project():
        pw = 2 * HEAD_DIM
        for c in range(o5_ref.shape[1] // 2):
            acc = jnp.dot(h_ref[...], wb_ref[:, c * pw:(c + 1) * pw], preferred_element_type=f32)
            ob_ref[:, c * pw:(c + 1) * pw] = acc.astype(bf16)
            o5_ref[:, 2 * c, :] = acc[:, :HEAD_DIM]
            o5_ref[:, 2 * c + 1, :] = acc[:, HEAD_DIM:]

    if fill:
        pl.when(pl.program_id(1) == layer)(project)

        @pl.when(pl.program_id(1) != layer)
        def _():
            o5_ref[...] = jnp.zeros_like(o5_ref)
    else:
        project()


def kv_matmul(h, w, layer, col0, nrows, depth, out_buf):
    n, kdim = h.shape
    ncols = N_HEADS * HEAD_DIM
    tm = _pick(nrows, (512, 256, 128, 64, 32))
    hpt = N_HEADS
    tn = hpt * HEAD_DIM
    coff = col0 // tn
    fill = out_buf is None
    last = nrows // tm - 1
    if fill:
        slab = lambda s: s
        row = lambda s, m: jnp.where(s == layer, m, last)
    else:
        slab = lambda s: layer
        row = lambda s, m: m
    in_specs = [pl.BlockSpec((tm, kdim), lambda j, s, m: (row(s, m), 0)),
                pl.BlockSpec((None, kdim, tn), lambda j, s, m: (layer, 0, j + coff),
                             pipeline_mode=pl.Buffered(1))]
    args = [h, w]
    kern = functools.partial(_kv_kernel, layer=layer, fill=fill)
    aliases = {}
    if not fill:
        in_specs.append(pl.BlockSpec(memory_space=pl.ANY))
        args.append(out_buf)
        aliases = {2: 0}
        kern = functools.partial(_drop_ref, kern, 2)
    return pl.pallas_call(
        kern,
        grid=(ncols // tn, depth if fill else 1, nrows // tm),
        in_specs=in_specs,
        out_specs=[pl.BlockSpec((None, tm, hpt, HEAD_DIM), lambda j, s, m: (slab(s), m, j, 0)),
                   pl.BlockSpec((tm, tn), lambda j, s, m: (row(s, m), j))],
        out_shape=[jax.ShapeDtypeStruct((depth, nrows, N_HEADS, HEAD_DIM), f32),
                   jax.ShapeDtypeStruct((nrows, ncols), bf16)],
        scratch_shapes=[pltpu.VMEM((kdim, tn), bf16)],
        input_output_aliases=aliases,
        compiler_params=_params(("arbitrary", "arbitrary", "arbitrary")),
        name="kv_matmul",
    )(*args)


def _pair_blockdiag(x2):
    lane = lax.broadcasted_iota(jnp.int32, x2.shape, 1)
    first = lane < HEAD_DIM
    zero = jnp.zeros_like(x2)
    return jnp.concatenate([jnp.where(first, x2, zero), jnp.where(first, zero, x2)], axis=0)


def _pair_cols(t0, t1, tq):
    return jnp.concatenate([jnp.broadcast_to(t0, (tq, LANES)), jnp.broadcast_to(t1, (tq, LANES))], axis=1)


def _sb_pair_block(q2, k2, v2, tri2, carry2, masked):
    tk = k2.shape[0]
    z = lax.dot_general(q2, _pair_blockdiag(k2), (((1,), (1,)), ((), ())), preferred_element_type=f32)
    lk = _log_sigmoid(-z)
    if masked:
        row = lax.broadcasted_iota(jnp.int32, z.shape, 0)
        col = lax.broadcasted_iota(jnp.int32, z.shape, 1)
        mask = jnp.where(col >= tk, col - tk, col) < row
        lkm = jnp.where(mask, lk, 0.0)
    else:
        lkm = lk
    hi = lkm.astype(bf16)
    lo = (lkm - hi.astype(f32)).astype(bf16)
    suf = (jnp.dot(hi, tri2, preferred_element_type=f32)
           + jnp.dot(lo, tri2, preferred_element_type=f32))
    arg = z + lk + suf
    if carry2 is not None:
        arg = arg + carry2
    w = jnp.exp(arg)
    if masked:
        w = jnp.where(mask, w, 0.0)
    out = jnp.dot(w.astype(bf16), _pair_blockdiag(v2), preferred_element_type=f32)
    t0 = jnp.sum(lkm[:, :tk], axis=1, keepdims=True)
    t1 = jnp.sum(lkm[:, tk:], axis=1, keepdims=True)
    return out, t0, t1


def _attn_kernel(*refs, tq, n_qblk, pairs, n_past_blk, layer):
    refs = list(refs)
    q_ref, k_ref, v_ref = refs[:3]
    pos = 3
    if n_past_blk:
        pk_hbm, pv_hbm = refs[3:5]
        pos = 5
    tri_ref = refs[pos]
    pos += 1
    trid_ref = tri_ref
    if tq != LANES:
        trid_ref = refs[pos]
        pos += 1
    o_ref = refs[pos]
    pos += 1
    acc = refs[pos:pos + pairs]
    car = refs[pos + pairs:pos + 2 * pairs]
    pos += 2 * pairs
    if n_past_blk:
        kbuf, vbuf, sem = refs[pos:pos + 3]
    pw = 2 * HEAD_DIM
    heads = 2 * pairs

    def ps(p):
        return slice(p * pw, (p + 1) * pw)

    def sweep(q2s, j0, load_kv, before=None):
        tri2 = tri_ref[...]

        def cond(st):
            j, mx = st
            return jnp.logical_and(j >= 0, mx > SB_EXIT)

        def body(st):
            j, _ = st
            if before is not None:
                before(j)
            mx = None
            for p in range(pairs):
                k2, v2 = load_kv(j, p)
                c2 = car[p][...]
                out, t0, t1 = _sb_pair_block(q2s[p], k2, v2, tri2, c2, False)
                acc[p][...] = acc[p][...] + out
                c2 = c2 + _pair_cols(t0, t1, tq)
                car[p][...] = c2
                m = jnp.max(c2)
                mx = m if mx is None else jnp.maximum(mx, m)
            return j - 1, mx

        mx0 = None
        for p in range(pairs):
            m = jnp.max(car[p][...])
            mx0 = m if mx0 is None else jnp.maximum(mx0, m)
        return lax.while_loop(cond, body, (j0, mx0))

    if n_past_blk:
        b = pl.program_id(0)
        h0 = pl.multiple_of(pl.program_id(1) * heads, heads)

        def cache_copies(j, slot):
            rows = pl.ds(pl.multiple_of(j * LANES, LANES), LANES)
            return (pltpu.make_async_copy(pk_hbm.at[layer, b, rows, pl.ds(h0, heads), :], kbuf.at[slot],
                                          sem.at[0, slot]),
                    pltpu.make_async_copy(pv_hbm.at[layer, b, rows, pl.ds(h0, heads), :], vbuf.at[slot],
                                          sem.at[1, slot]))

        def cache_start(j, slot):
            for c in cache_copies(j, slot):
                c.start()

        def cache_wait(j, slot):
            for c in cache_copies(j, slot):
                c.wait()

    def qblock(qb, _):
        r0 = pl.multiple_of(qb * tq, tq)
        if n_past_blk:
            cache_start(n_past_blk - 1, (n_past_blk - 1) % 2)
        q2s = [q_ref[pl.ds(r0, tq), ps(p)] for p in range(pairs)]
        trid = trid_ref[...]
        for p in range(pairs):
            kd = k_ref[pl.ds(r0, tq), ps(p)].astype(bf16)
            vd = v_ref[pl.ds(r0, tq), ps(p)].astype(bf16)
            out, t0, t1 = _sb_pair_block(q2s[p], kd, vd, trid, None, True)
            acc[p][...] = out
            car[p][...] = _pair_cols(t0, t1, tq)

        if n_qblk > 1:
            def load_own(j, p):
                rj = pl.multiple_of(j * tq, tq)
                return (k_ref[pl.ds(rj, tq), ps(p)].astype(bf16),
                        v_ref[pl.ds(rj, tq), ps(p)].astype(bf16))
            sweep(q2s, qb - 1, load_own)

        if n_past_blk:
            def before(j):
                cache_wait(j, j % 2)

                @pl.when(j >= 1)
                def _():
                    cache_start(j - 1, (j - 1) % 2)

            def load_past(j, p):
                kb = kbuf.at[j % 2]
                vb = vbuf.at[j % 2]
                k2 = jnp.concatenate([kb[:, 2 * p, :], kb[:, 2 * p + 1, :]], axis=1)
                v2 = jnp.concatenate([vb[:, 2 * p, :], vb[:, 2 * p + 1, :]], axis=1)
                return k2.astype(bf16), v2.astype(bf16)

            j_end, _ = sweep(q2s, n_past_blk - 1, load_past, before)

            @pl.when(j_end >= 0)
            def _():
                cache_wait(j_end, j_end % 2)

        for p in range(pairs):
            o_ref[pl.ds(r0, tq), ps(p)] = acc[p][...].astype(o_ref.dtype)
        return 0

    if n_qblk == 1:
        qblock(0, 0)
    else:
        lax.fori_loop(0, n_qblk, qblock, 0)


def _attn_wave_kernel(q_ref, k_ref, v_ref, tri_ref, o_ref, *scratch, n_qblk, pairs, win):
    acc = scratch[:pairs]
    car = scratch[pairs:2 * pairs]
    tq = LANES
    pw = 2 * HEAD_DIM

    def ps(p):
        return slice(p * pw, (p + 1) * pw)

    def step(j, nact):
        r0 = _aligned(j * tq, tq)
        tri2 = tri_ref[...]
        row = lax.broadcasted_iota(jnp.int32, (tq, 2 * tq), 0)
        col = lax.broadcasted_iota(jnp.int32, (tq, 2 * tq), 1)
        mask = jnp.where(col >= tq, col - tq, col) < row
        for p in range(pairs):
            q_all = q_ref[pl.ds(r0, nact * tq), ps(p)]
            kk = _pair_blockdiag(k_ref[pl.ds(r0, tq), ps(p)])
            vv = _pair_blockdiag(v_ref[pl.ds(r0, tq), ps(p)])
            z = lax.dot_general(q_all, kk, (((1,), (1,)), ((), ())), preferred_element_type=f32)
            his, los, args, tots = [], [], [], []
            for i in range(nact):
                zi = z[i * tq:(i + 1) * tq]
                lk = _log_sigmoid(-zi)
                lkm = jnp.where(mask, lk, 0.0) if i == 0 else lk
                hi = lkm.astype(bf16)
                his.append(hi)
                los.append((lkm - hi.astype(f32)).astype(bf16))
                args.append(zi + lk)
                tots.append(_pair_cols(jnp.sum(lkm[:, :tq], axis=1, keepdims=True),
                                       jnp.sum(lkm[:, tq:], axis=1, keepdims=True), tq))
            suf = (jnp.dot(jnp.concatenate(his, axis=0), tri2, preferred_element_type=f32)
                   + jnp.dot(jnp.concatenate(los, axis=0), tri2, preferred_element_type=f32))
            ws = []
            for i in range(nact):
                a = args[i] + suf[i * tq:(i + 1) * tq]
                if i == 0:
                    w = jnp.where(mask, jnp.exp(a), 0.0)
                else:
                    w = jnp.exp(a + car[p][(j + i) % win])
                ws.append(w.astype(bf16))
            out = jnp.dot(jnp.concatenate(ws, axis=0), vv, preferred_element_type=f32)
            for i in range(nact):
                slot = (j + i) % win
                if i == 0:
                    acc[p][slot] = out[:tq]
                    car[p][slot] = tots[0]
                else:
                    acc[p][slot] = acc[p][slot] + out[i * tq:(i + 1) * tq]
                    car[p][slot] = car[p][slot] + tots[i]

    def finalize(qb, j0):
        slot = qb % win
        rq = _aligned(qb * tq, tq)
        if not (isinstance(j0, int) and j0 < 0):
            tri2 = tri_ref[...]
            q2s = [q_ref[pl.ds(rq, tq), ps(p)] for p in range(pairs)]

            def cond(st):
                j, mx = st
                return jnp.logical_and(j >= 0, mx > SB_EXIT)

            def body(st):
                j, _ = st
                rj = pl.multiple_of(j * tq, tq)
                mx = None
                for p in range(pairs):
                    c2 = car[p][slot]
                    out, t0, t1 = _sb_pair_block(q2s[p], k_ref[pl.ds(rj, tq), ps(p)], v_ref[pl.ds(rj, tq), ps(p)],
                                                 tri2, c2, False)
                    acc[p][slot] = acc[p][slot] + out
                    c2 = c2 + _pair_cols(t0, t1, tq)
                    car[p][slot] = c2
                    m = jnp.max(c2)
                    mx = m if mx is None else jnp.maximum(mx, m)
                return j - 1, mx

            mx0 = None
            for p in range(pairs):
                m = jnp.max(car[p][slot])
                mx0 = m if mx0 is None else jnp.maximum(mx0, m)
            lax.while_loop(cond, body, (j0, mx0))
        for p in range(pairs):
            o_ref[pl.ds(rq, tq), ps(p)] = acc[p][slot].astype(o_ref.dtype)

    n_full = max(n_qblk - win + 1, 0)
    for j in range(n_qblk - 1, n_full - 1, -1):
        step(j, n_qblk - j)

    if n_full:
        def full_step(i, _):
            j = n_full - 1 - i
            step(j, win)
            finalize(j + win - 1, j - 1)
            return 0
        lax.fori_loop(0, n_full, full_step, 0)

    for qb in range(min(win - 1, n_qblk) - 1, -1, -1):
        finalize(qb, -1)


def _tri_pair(t):
    tri = (jnp.arange(t)[:, None] > jnp.arange(t)[None, :]).astype(bf16)
    z = jnp.zeros((t, t), bf16)
    return jnp.concatenate([jnp.concatenate([tri, z], axis=1), jnp.concatenate([z, tri], axis=1)], axis=0)


def sb_attention(q, k, v, *, batch, seq, row0, layer=0, past_k=None, past_v=None):
    n, width = q.shape
    tq = min(seq, LANES)
    n_qblk = seq // tq
    heads = ATTN_HEADS
    pairs = heads // 2
    gw = heads * HEAD_DIM
    assert seq % tq == 0 and row0 % seq == 0 and width % gw == 0
    boff = row0 // seq
    in_specs = [pl.BlockSpec((seq, gw), lambda b, h: (b + boff, h)),
                pl.BlockSpec((seq, gw), lambda b, h: (b, h)),
                pl.BlockSpec((seq, gw), lambda b, h: (b, h))]
    args = [q, k, v]
    scratch = ([pltpu.VMEM((tq, 2 * HEAD_DIM), f32) for _ in range(pairs)]
               + [pltpu.VMEM((tq, 2 * LANES), f32) for _ in range(pairs)])
    n_past_blk = 0
    if past_k is not None:
        past = past_k.shape[2]
        assert past % LANES == 0
        n_past_blk = past // LANES
        in_specs += [pl.BlockSpec(memory_space=pl.ANY), pl.BlockSpec(memory_space=pl.ANY)]
        args += [past_k, past_v]
        scratch += [pltpu.VMEM((2, LANES, heads, HEAD_DIM), f32),
                    pltpu.VMEM((2, LANES, heads, HEAD_DIM), f32),
                    pltpu.SemaphoreType.DMA((2, 2))]
    in_specs.append(pl.BlockSpec((2 * LANES, 2 * LANES), lambda b, h: (0, 0)))
    args.append(_tri_pair(LANES))
    if tq != LANES:
        in_specs.append(pl.BlockSpec((2 * tq, 2 * tq), lambda b, h: (0, 0)))
        args.append(_tri_pair(tq))
    if past_k is None and n_qblk > 1:
        win = ATTN_WINDOW
        scratch = ([pltpu.VMEM((win, tq, 2 * HEAD_DIM), f32) for _ in range(pairs)]
                   + [pltpu.VMEM((win, tq, 2 * LANES), f32) for _ in range(pairs)])
        kern = functools.partial(_attn_wave_kernel, n_qblk=n_qblk, pairs=pairs, win=win)
    else:
        kern = functools.partial(_attn_kernel, tq=tq, n_qblk=n_qblk, pairs=pairs, n_past_blk=n_past_blk,
                                 layer=layer)
    return pl.pallas_call(
        kern,
        grid=(batch, width // gw),
        in_specs=in_specs,
        out_specs=pl.BlockSpec((seq, gw), lambda b, h: (b, h)),
        out_shape=jax.ShapeDtypeStruct((batch * seq, width), bf16),
        scratch_shapes=scratch,
        compiler_params=_params(("parallel", "parallel")),
        name="sb_attention",
    )(*args)


def _one_minus_sq(log_a, a):
    y = 2.0 * log_a
    series = -y * (1.0 + y * (1.0 / 2 + y * (1.0 / 6)))
    return jnp.where(y > -0.01, series, 1.0 - a * a)


def _lru_kernel(xb_ref, yg_ref, cprev_ref, h0_ref, cw_ref, cb_ref, wa_ref, wx_ref, ba_ref, bx_ref,
                lam_ref, ob_ref, cnew_ref, hlast_ref, xpad_s, a_s, u_s, hs_s, h_s, *, tt, n_chunks):
    tc = pl.program_id(1)
    npad = SUBLANES
    nprev = CONV_W - 1

    @pl.when(tc == 0)
    def _():
        xpad_s[npad - nprev:npad, :] = cprev_ref[...]
        h_s[...] = h0_ref[...]

    @pl.when(tc > 0)
    def _():
        xpad_s[npad - nprev:npad, :] = xpad_s[npad + tt - nprev:npad + tt, :]

    xpad_s[npad:npad + tt, :] = xb_ref[...]
    xc = jnp.broadcast_to(cb_ref[...], (tt, cb_ref.shape[1]))
    for tap in range(CONV_W):
        s = npad - nprev + tap
        xc = xc + xpad_s[s:s + tt, :] * cw_ref[tap:tap + 1, :]

    log_sig_lam = _log_sigmoid(lam_ref[...])
    bw = LANES
    for nb in range(LRU_BLOCKS):
        cs = slice(nb * bw, (nb + 1) * bw)
        xcb = xc[:, cs]
        xcb16 = xcb.astype(bf16)
        r = jax.nn.sigmoid(jnp.dot(xcb16, wa_ref[nb], preferred_element_type=f32) + ba_ref[:, cs])
        i = jax.nn.sigmoid(jnp.dot(xcb16, wx_ref[nb], preferred_element_type=f32) + bx_ref[:, cs])
        log_a = (LRU_C * r) * log_sig_lam[:, cs]
        a = jnp.exp(log_a)
        a_s[:, cs] = a
        u_s[:, cs] = jnp.sqrt(_one_minus_sq(log_a, a)) * (i * xcb)

    def step(t, h):
        h = a_s[pl.ds(t, 1), :] * h + u_s[pl.ds(t, 1), :]
        hs_s[pl.ds(t, 1), :] = h
        return h

    h = lax.fori_loop(0, tt, step, h_s[...], unroll=8)
    h_s[...] = h
    ob_ref[...] = (jax.nn.gelu(yg_ref[...].astype(f32)) * hs_s[...]).astype(ob_ref.dtype)

    @pl.when(tc == n_chunks - 1)
    def _():
        cnew_ref[...] = xpad_s[npad + tt - nprev:npad + tt, :]
        hlast_ref[...] = h


def rg_lru_mixer(xb, yg, conv_prev, h0, conv_w, conv_b, wa, wx, ba, bx, lam, *, batch, seq, row0):
    n, width = xb.shape
    tt = min(seq, 256)
    n_chunks = seq // tt
    assert seq % tt == 0 and row0 % tt == 0
    coff = row0 // tt
    row_spec = pl.BlockSpec((tt, width), lambda b, c: (coff + b * n_chunks + c, 0))
    out_row_spec = pl.BlockSpec((tt, width), lambda b, c: (b * n_chunks + c, 0))
    vec = lambda r: pl.BlockSpec((r, width), lambda b, c: (0, 0))
    blk = pl.BlockSpec((LRU_BLOCKS, LANES, LANES), lambda b, c: (0, 0, 0))
    in_specs = [row_spec, row_spec,
                pl.BlockSpec((None, CONV_W - 1, width), lambda b, c: (b, 0, 0)),
                pl.BlockSpec((None, 1, width), lambda b, c: (b, 0, 0)),
                vec(CONV_W), vec(1), blk, blk, vec(1), vec(1), vec(1)]
    args = [xb, yg, conv_prev, h0.reshape(batch, 1, width), conv_w, conv_b.reshape(1, width),
            wa.astype(bf16), wx.astype(bf16), ba.reshape(1, width), bx.reshape(1, width),
            lam.reshape(1, width)]
    ob, cnew, hlast = pl.pallas_call(
        functools.partial(_lru_kernel, tt=tt, n_chunks=n_chunks),
        grid=(batch, n_chunks),
        in_specs=in_specs,
        out_specs=[out_row_spec,
                   pl.BlockSpec((None, CONV_W - 1, width), lambda b, c: (b, 0, 0)),
                   pl.BlockSpec((None, 1, width), lambda b, c: (b, 0, 0))],
        out_shape=[jax.ShapeDtypeStruct((batch * seq, width), bf16),
                   jax.ShapeDtypeStruct((batch, CONV_W - 1, width), f32),
                   jax.ShapeDtypeStruct((batch, 1, width), f32)],
        scratch_shapes=[pltpu.VMEM((SUBLANES + tt, width), f32),
                        pltpu.VMEM((tt, width), f32),
                        pltpu.VMEM((tt, width), f32),
                        pltpu.VMEM((tt, width), f32),
                        pltpu.VMEM((1, width), f32)],
        compiler_params=_params(("parallel", "arbitrary")),
        name="rg_lru_mixer",
    )(*args)
    return ob, cnew, hlast.reshape(batch, width)


def _mix_kernel(oap_ref, oas_ref, obp_ref, obs_ref, ga_ref, gb_ref, wpa_ref, wpb_ref, o_ref, wa_s, wb_s,
                *, prompt_tiles):
    m = pl.program_id(1)

    @pl.when(m == 0)
    def _():
        wa_s[...] = wpa_ref[...].astype(bf16)
        wb_s[...] = wpb_ref[...].astype(bf16)

    def mix(oa_ref, ob_ref):
        pa = jnp.dot(oa_ref[...], wa_s[...], preferred_element_type=f32)
        pb = jnp.dot(ob_ref[...], wb_s[...], preferred_element_type=f32)
        mixed = (jax.nn.sigmoid(ga_ref[...].astype(f32)) * pa
                 + jax.nn.sigmoid(gb_ref[...].astype(f32)) * pb)
        o_ref[...] = mixed.astype(o_ref.dtype)

    pl.when(m < prompt_tiles)(functools.partial(mix, oap_ref, obp_ref))
    pl.when(m >= prompt_tiles)(functools.partial(mix, oas_ref, obs_ref))


def gated_mix(oa_p, oa_s, ob_p, ob_s, ga, gb, w_pa, w_pb, layer):
    n_p, kdim = oa_p.shape
    n_s = oa_s.shape[0]
    ncols = w_pa.shape[2]
    tm = _pick(n_s, (256, 128, 64, 32))
    assert n_p % tm == 0
    pt = n_p // tm
    tn = _pick(ncols, (1024, 512, 256, 128))
    act_p =pl.BlockSpec((tm, kdim), lambda j, m: (jnp.minimum(m, pt - 1), 0))
    act_s = pl.BlockSpec((tm, kdim), lambda j, m: (jnp.maximum(m - pt, 0), 0))
    gate = pl.BlockSpec((tm, tn), lambda j, m: (m, j))
    wsp = pl.BlockSpec((None, kdim, tn), lambda j, m: (layer, 0, j))
    return pl.pallas_call(
        functools.partial(_mix_kernel, prompt_tiles=pt),
        grid=(ncols // tn, (n_p + n_s) // tm),
        in_specs=[act_p, act_s, act_p, act_s, gate, gate, wsp, wsp],
        out_specs=gate,
        out_shape=jax.ShapeDtypeStruct((n_p + n_s, ncols), bf16),
        scratch_shapes=[pltpu.VMEM((kdim, tn), bf16), pltpu.VMEM((kdim, tn), bf16)],
        compiler_params=_params(("arbitrary", "arbitrary")),
        name="gated_mix",
    )(oa_p, oa_s, ob_p, ob_s, ga, gb, w_pa, w_pb)


def _route(logits, forced_group=None):
    lane = lax.broadcasted_iota(jnp.int32, logits.shape, 1).astype(f32)
    neg = jnp.float32(-3.0e38)
    big = jnp.float32(LANES)

    def first_argmax(vals):
        m = jnp.max(vals, axis=1, keepdims=True)
        idx = jnp.min(jnp.where(vals == m, lane, big), axis=1, keepdims=True)
        return m, idx

    group_lanes = lane < N_GROUPS
    gl = jnp.where(group_lanes, logits, neg)
    gmax, gidx = first_argmax(gl)
    if forced_group is None:
        g_logit = gmax
    else:
        gidx = jnp.full_like(gidx, forced_group)
        g_logit = jnp.sum(jnp.where(lane == gidx, logits, 0.0), axis=1, keepdims=True)
    p_g = jnp.exp(g_logit - gmax) / jnp.sum(jnp.where(group_lanes, jnp.exp(gl - gmax), 0.0), axis=1,
                                             keepdims=True)
    lo = ROUTE_EXPERT_LANE0 + EXPERTS_PER_GROUP * gidx
    sel = jnp.logical_and(lane >= lo, lane < lo + EXPERTS_PER_GROUP)
    sl = jnp.where(sel, logits, neg)
    m1, i1 = first_argmax(sl)
    sl2 = jnp.where(lane == i1, neg, sl)
    m2, i2 = first_argmax(sl2)
    e2 = jnp.exp(m2 - m1)
    den = 1.0 + e2
    w1 = (1.0 / den) * p_g
    w2 = (e2 / den) * p_g
    comb = jnp.where(lane == i1, w1, 0.0) + jnp.where(lane == i2, w2, 0.0)
    return gidx, comb


def _router_weights(wg, bg, ws, bs):
    d = wg.shape[0]
    pad = LANES - N_GROUPS - N_EXPERTS
    wsub = jnp.transpose(ws, (1, 0, 2)).reshape(d, N_EXPERTS)
    wr = jnp.concatenate([wg, wsub, jnp.zeros((d, pad), f32)], axis=1).astype(bf16)
    br = jnp.concatenate([bg, bs.reshape(-1), jnp.zeros((pad,), f32)]).reshape(1, LANES)
    return wr, br


def _router_kernel(x_ref, g_ref, wr_ref, br_ref, slab_ref, grp_ref):
    h = _rmsnorm_val(x_ref[...], g_ref[...]).astype(bf16)
    logits = jnp.dot(h, wr_ref[...], preferred_element_type=f32) + br_ref[...]
    gidx, _ = _route(logits)
    grp_ref[...] = jnp.broadcast_to(gidx, grp_ref.shape)
    _slab_store(slab_ref, h.astype(f32))


def router(x, g, wr, br):
    n, d = x.shape
    tm = _pick(n, (256, 128, 64, 32))
    s_per = d // LANES
    return pl.pallas_call(
        _router_kernel,
        grid=(n // tm,),
        in_specs=[pl.BlockSpec((tm, d), lambda m: (m, 0)),
                  pl.BlockSpec((1, d), lambda m: (0, 0)),
                  pl.BlockSpec((d, LANES), lambda m: (0, 0)),
                  pl.BlockSpec((1, LANES), lambda m: (0, 0))],
        out_specs=[pl.BlockSpec((tm * s_per, LANES), lambda m: (m, 0)),
                   pl.BlockSpec((tm, LANES), lambda m: (m, 0))],
        out_shape=[jax.ShapeDtypeStruct((n * s_per, LANES), f32),
                   jax.ShapeDtypeStruct((n, LANES), f32)],
        compiler_params=_params(("parallel",)),
        name="router",
    )(x, g.reshape(1, d), wr, br)


def _group_layout(grp, tm):
    n = grp.shape[0]
    g = grp[:, 0].astype(jnp.int32)
    onehot = (g[:, None] == jnp.arange(N_GROUPS, dtype=jnp.int32)[None, :]).astype(jnp.int32)
    csum = jnp.cumsum(onehot, axis=0)
    counts = csum[-1]
    rank = jnp.sum(onehot * csum, axis=1) - 1
    tiles = (counts + tm - 1) // tm
    tile_end = jnp.cumsum(tiles)
    tile_start = tile_end - tiles
    pos = (jnp.sum(onehot * tile_start[None, :], axis=1) * tm + rank).astype(jnp.int32)
    nt_max = (n + N_GROUPS * (tm - 1)) // tm
    src = jnp.zeros((nt_max * tm,), jnp.int32).at[pos].set(jnp.arange(n, dtype=jnp.int32))
    tix = jnp.arange(nt_max, dtype=jnp.int32)
    tile_group = jnp.minimum(jnp.sum((tix[:, None] >= tile_end[None, :]).astype(jnp.int32), axis=1),
                             N_GROUPS - 1).astype(jnp.int32)
    return src, pos, tile_group, tile_end[-1:].astype(jnp.int32)


def _moe_kernel(src_ref, tg_ref, nt_ref, slab_hbm, wr_ref, br_ref, wu_ref, wd_ref, ys_ref,
                gbuf, sem, lhs_s, comb_s, acc_s, *, tm, s_per, ff, rows):
    t = pl.program_id(0)
    e = pl.program_id(1)
    nt = nt_ref[0]
    last_e = EXPERTS_PER_GROUP - 1
    nxt = jnp.minimum(t + 1, nt - 1)

    def row_copy(row, i):
        return pltpu.make_async_copy(slab_hbm.at[pl.ds(pl.multiple_of(row * s_per, s_per), s_per), :],
                                     gbuf.at[pl.ds(pl.multiple_of(i * s_per, s_per), s_per), :],
                                     sem.at[0])

    def issue(tile):
        def body(i, _):
            row_copy(src_ref[tile * tm + i], i).start()
            return 0
        lax.fori_loop(0, tm, body, 0, unroll=8)

    def wait_tile():
        pltpu.make_async_copy(slab_hbm.at[pl.ds(0, tm * s_per), :], gbuf, sem.at[0]).wait()

    def prepare(tile):
        buf = tile % 2
        for s in range(s_per):
            lhs_s[buf, :, s * LANES:(s + 1) * LANES] = gbuf[pl.ds(s, tm, stride=s_per), :].astype(bf16)
        logits = jnp.dot(lhs_s[buf], wr_ref[...], preferred_element_type=f32) + br_ref[...]
        _, comb = _route(logits, tg_ref[tile].astype(f32))
        comb_s[buf] = comb

    def product(first):
        buf = t % 2
        lane_e = ROUTE_EXPERT_LANE0 + EXPERTS_PER_GROUP * tg_ref[t] + e
        for r in range(tm // rows):
            rs = slice(r * rows, (r + 1) * rows)
            comb = comb_s[buf, rs, :]
            lane = lax.broadcasted_iota(jnp.int32, comb.shape, 1)
            c = jnp.sum(jnp.where(lane == lane_e, comb, 0.0), axis=1, keepdims=True)
            up = jnp.dot(lhs_s[buf, rs, :], wu_ref[...], preferred_element_type=f32)
            act = (jax.nn.silu(up[:, :ff]) * up[:, ff:]).astype(bf16)
            y = c * jnp.dot(act, wd_ref[...], preferred_element_type=f32)
            acc_s[rs, :] = y if first else acc_s[rs, :] + y

    @pl.when(jnp.logical_and(e == 0, t < nt))
    def _():
        @pl.when(t == 0)
        def _():
            issue(0)
            wait_tile()
            prepare(0)

        issue(nxt)
        product(True)

    @pl.when(jnp.logical_and(jnp.logical_and(e > 0, e < last_e), t < nt))
    def _():
        product(False)

    @pl.when(jnp.logical_and(e == last_e, t < nt))
    def _():
        wait_tile()
        product(False)
        prepare(nxt)
        _slab_store(ys_ref, acc_s)

    @pl.when(jnp.logical_and(e == last_e, t >= nt))
    def _():
        ys_ref[...] = jnp.zeros_like(ys_ref)


def moe_sorted(slab, src, tile_group, nt, wr, br, wu, wd, layer, d):
    tm = MOE_TILE
    s_per = d // LANES
    nt_max = src.shape[0] // tm
    ff2 = wu.shape[3]
    ff = ff2 // 2
    rows = 256

    def widx(t, e, src_r, tg_r, nt_r):
        return tg_r[t] * EXPERTS_PER_GROUP + e

    grid_spec = pltpu.PrefetchScalarGridSpec(
        num_scalar_prefetch=3,
        grid=(nt_max, EXPERTS_PER_GROUP),
        in_specs=[pl.BlockSpec(memory_space=pl.ANY),
                  pl.BlockSpec((d, LANES), lambda t, e, *_: (0, 0)),
                  pl.BlockSpec((1, LANES), lambda t, e, *_: (0, 0)),
                  pl.BlockSpec((None, None, d, ff2), lambda t, e, *s: (layer, widx(t, e, *s), 0, 0)),
                  pl.BlockSpec((None, None, ff, d), lambda t, e, *s: (layer, widx(t, e, *s), 0, 0))],
        out_specs=pl.BlockSpec((tm * s_per, LANES), lambda t, e, *_: (t, 0)),
        scratch_shapes=[pltpu.VMEM((tm * s_per, LANES), f32),
                        pltpu.SemaphoreType.DMA((1,)),
                        pltpu.VMEM((2, tm, d), bf16),
                        pltpu.VMEM((2, tm, LANES), f32),
                        pltpu.VMEM((tm, d), f32)])
    return pl.pallas_call(
        functools.partial(_moe_kernel, tm=tm, s_per=s_per, ff=ff, rows=rows),
        grid_spec=grid_spec,
        out_shape=jax.ShapeDtypeStruct((nt_max * tm * s_per, LANES), f32),
        compiler_params=_params(("arbitrary", "arbitrary")),
        name="moe_sorted",
    )(src, tile_group, nt, slab, wr, br, wu, wd)


def _unperm_kernel(pos_ref, x1_ref, g_ref, ys_hbm, *rest, tm, s_per, final, tok0):
    if final:
        y_ref, gbuf, sem = rest
    else:
        x_ref, h_ref, gbuf, sem = rest
    m = pl.program_id(0)
    nm = pl.num_programs(0)

    def row_copy(row, i, slot):
        return pltpu.make_async_copy(ys_hbm.at[pl.ds(pl.multiple_of(row * s_per, s_per), s_per), :],
                                     gbuf.at[slot, pl.ds(pl.multiple_of(i * s_per, s_per), s_per), :],
                                     sem.at[slot])

    def issue(tile, slot):
        def body(i, _):
            row_copy(pos_ref[tok0 + tile * tm + i], i, slot).start()
            return 0
        lax.fori_loop(0, tm, body, 0, unroll=8)

    def wait_tile(slot):
        pltpu.make_async_copy(ys_hbm.at[pl.ds(0, tm * s_per), :], gbuf.at[slot], sem.at[slot]).wait()

    @pl.when(m == 0)
    def _():
        issue(0, 0)

    @pl.when(m + 1 < nm)
    def _():
        issue(m + 1, (m + 1) % 2)

    wait_tile(m % 2)
    x = x1_ref[...] + _slab_load(gbuf.at[m % 2], tm, s_per)
    if final:
        y_ref[...] = _rmsnorm_val(x, g_ref[...])
    else:
        x_ref[...] = x
        h_ref[...] = _rmsnorm_val(x, g_ref[...]).astype(h_ref.dtype)


def add_experts_norm(x1, ys, pos, g, *, final, row0=0, nrows=None):
    n, d = x1.shape
    nrows = n - row0 if nrows is None else nrows
    tm = _pick(nrows, (256, 128, 64, 32))
    assert row0 % tm == 0
    off = row0 // tm
    s_per = d // LANES
    row_in = pl.BlockSpec((tm, d), lambda m, *_: (m + off, 0))
    row_out = pl.BlockSpec((tm, d), lambda m, *_: (m, 0))
    if final:
        out_specs = row_out
        out_shape = jax.ShapeDtypeStruct((nrows, d), f32)
    else:
        out_specs = [row_out, row_out]
        out_shape = [jax.ShapeDtypeStruct((nrows, d), f32), jax.ShapeDtypeStruct((nrows, d), bf16)]
    grid_spec = pltpu.PrefetchScalarGridSpec(
        num_scalar_prefetch=1,
        grid=(nrows // tm,),
        in_specs=[row_in,
                  pl.BlockSpec((1, d), lambda m, *_: (0, 0)),
                  pl.BlockSpec(memory_space=pl.ANY)],
        out_specs=out_specs,
        scratch_shapes=[pltpu.VMEM((2, tm * s_per, LANES), f32),
                        pltpu.SemaphoreType.DMA((2,))])
    return pl.pallas_call(
        functools.partial(_unperm_kernel, tm=tm, s_per=s_per, final=final, tok0=row0),
        grid_spec=grid_spec,
        out_shape=out_shape,
        compiler_params=_params(("arbitrary",)),
        name="add_experts_norm",
    )(pos, x1, g.reshape(1, d), ys)


def kernel(x_prompt, x_sample, cache_k, cache_v, state_conv, state_lru, norm1_g, w_in, conv_w, conv_b,
           lru_wa, lru_ba, lru_wx, lru_bx, lru_lambda, w_pa, w_pb, w_o, norm2_g, router_group_w,
           router_group_b, router_sub_w, router_sub_b, expert_w_up, expert_w_down, final_norm_g):
    bp, tp, d = x_prompt.shape
    bs, ts, _ = x_sample.shape
    depth = w_in.shape[0]
    sbw = N_HEADS * HEAD_DIM
    lw = conv_w.shape[2]
    n_p, n_s = bp * tp, bs * ts

    x, h1 = join_norm(x_prompt.reshape(n_p, d), x_sample.reshape(n_s, d), norm1_g[0])
    zero_conv = jnp.zeros((bp, CONV_W - 1, lw), f32)
    zero_lru = jnp.zeros((bp, lw), f32)
    wu16 = expert_w_up.astype(bf16)
    wd16 = expert_w_down.astype(bf16)

    kp = vp = None
    ks_l, vs_l, cp_l, cs_l, hp_l, hs_l = [], [], [], [], [], []
    for l in range(depth):
        q = matmul(h1, w_in, l, 0, sbw, bf16, scale=SB_SCALE)
        kp, kb = kv_matmul(h1, w_in, l, sbw, n_p, depth, kp)
        vp, vb = kv_matmul(h1, w_in, l, 2 * sbw, n_p, depth, vp)
        ks = matmul(h1, w_in, l, sbw, sbw, f32, row0=n_p, nrows=n_s)
        vs = matmul(h1, w_in, l, 2 * sbw, sbw, f32, row0=n_p, nrows=n_s)
        c0 = 3 * sbw
        xb = matmul(h1, w_in, l, c0, lw, f32)
        yg = matmul(h1, w_in, l, c0 + lw, lw, bf16)
        ga = matmul(h1, w_in, l, c0 + 2 * lw, d, bf16)
        gb = matmul(h1, w_in, l, c0 + 2 * lw + d, d, bf16)

        oa_p = sb_attention(q, kb, vb, batch=bp, seq=tp, row0=0)
        oa_s = sb_attention(q, ks, vs, batch=bs, seq=ts, row0=n_p, layer=l, past_k=cache_k, past_v=cache_v)

        lru_args = (conv_w[l], conv_b[l], lru_wa[l], lru_wx[l], lru_ba[l], lru_bx[l], lru_lambda[l])
        ob_p, cp, hp = rg_lru_mixer(xb, yg, zero_conv, zero_lru, *lru_args, batch=bp, seq=tp, row0=0)
        ob_s, cs, hs = rg_lru_mixer(xb, yg, state_conv[l], state_lru[l], *lru_args, batch=bs, seq=ts,
                                    row0=n_p)

        mixed = gated_mix(oa_p, oa_s, ob_p, ob_s, ga, gb, w_pa, w_pb, l)
        x1 = matmul(mixed, w_o, l, 0, d, f32, res=x)

        wr, br = _router_weights(router_group_w[l], router_group_b[l], router_sub_w[l], router_sub_b[l])
        slab, grp = router(x1, norm2_g[l], wr, br)
        src, pos, tile_group, nt = _group_layout(grp, MOE_TILE)
        ys = moe_sorted(slab, src, tile_group, nt, wr, br, wu16, wd16, l, d)
        if l + 1 < depth:
            x, h1 = add_experts_norm(x1, ys, pos, norm1_g[l + 1], final=False)
        else:
            y_prompt = add_experts_norm(x1, ys, pos, final_norm_g, final=True, row0=0, nrows=n_p)
            y_sample = add_experts_norm(x1, ys, pos, final_norm_g, final=True, row0=n_p, nrows=n_s)

        ks_l.append(ks)
        vs_l.append(vs)
        cp_l.append(cp)
        cs_l.append(cs)
        hp_l.append(hp)
        hs_l.append(hs)

    prompt_k = kp.reshape(depth, bp, tp, N_HEADS, HEAD_DIM)
    prompt_v = vp.reshape(depth, bp, tp, N_HEADS, HEAD_DIM)
    sample_k = jnp.stack(ks_l).reshape(depth, bs, ts, N_HEADS, HEAD_DIM)
    sample_v = jnp.stack(vs_l).reshape(depth, bs, ts, N_HEADS, HEAD_DIM)
    return (y_prompt.reshape(bp, tp, d), y_sample.reshape(bs, ts, d), prompt_k, prompt_v,
            jnp.stack(cp_l), jnp.stack(hp_l), sample_k, sample_v, jnp.stack(cs_l), jnp.stack(hs_l))
```

```python
import functools

import jax
import jax.numpy as jnp
from jax import lax
from jax.experimental import pallas as pl
from jax.experimental.pallas import tpu as pltpu

N_HEADS = 16
HEAD_DIM = 128
SB_SCALE = HEAD_DIM ** -0.5
LRU_BLOCKS = 16
CONV_W = 4
LRU_C = 8.0
N_GROUPS = 4
EXPERTS_PER_GROUP = 4
N_EXPERTS = N_GROUPS * EXPERTS_PER_GROUP
NORM_EPS = 1e-6

LANES = 128
SUBLANES = 8
VMEM_LIMIT = 56 * 1024 * 1024
SB_EXIT = -110.0
ATTN_HEADS = 8
ATTN_WINDOW = 3
ROUTE_EXPERT_LANE0 = 4
MOE_TILE = 768

f32 = jnp.float32
bf16 = jnp.bfloat16


def _pick(n, cands):
    for c in cands:
        if n % c == 0:
            return c
    raise ValueError(f"no tile size for {n}")


def _params(sem):
    return pltpu.CompilerParams(dimension_semantics=sem, vmem_limit_bytes=VMEM_LIMIT)


def _drop_ref(kern, pos, *refs):
    kern(*(refs[:pos] + refs[pos + 1:]))


def _aligned(x, m):
    return x if isinstance(x, int) else pl.multiple_of(x, m)


def _log_sigmoid(x):
    return jnp.minimum(x, 0.0) - jnp.log(1.0 + jnp.exp(-jnp.abs(x)))


def _slab_store(ref, val):
    tm, d = val.shape
    s_per = d // LANES
    for s in range(s_per):
        ref[pl.ds(s, tm, stride=s_per), :] = val[:, s * LANES:(s + 1) * LANES]


def _slab_load(ref, tm, s_per):
    return jnp.concatenate([ref[pl.ds(s, tm, stride=s_per), :] for s in range(s_per)], axis=1)


def _rmsnorm_val(x, g):
    ms = jnp.mean(x * x, axis=-1, keepdims=True)
    return (x * lax.rsqrt(ms + NORM_EPS)) * g


def _join_norm_kernel(xp_ref, xs_ref, g_ref, x_ref, h_ref, *, prompt_tiles):
    def emit(src_ref):
        x = src_ref[...]
        x_ref[...] = x
        h_ref[...] = _rmsnorm_val(x, g_ref[...]).astype(h_ref.dtype)

    pl.when(pl.program_id(0) < prompt_tiles)(functools.partial(emit, xp_ref))
    pl.when(pl.program_id(0) >= prompt_tiles)(functools.partial(emit, xs_ref))


def join_norm(x_p, x_s, g):
    n_p, d = x_p.shape
    n_s = x_s.shape[0]
    tm = _pick(n_s, (256, 128, 64, 32, 16, 8))
    assert n_p % tm == 0
    pt = n_p // tm
    row_out = pl.BlockSpec((tm, d), lambda m: (m, 0))
    return pl.pallas_call(
        functools.partial(_join_norm_kernel, prompt_tiles=pt),
        grid=((n_p + n_s) // tm,),
        in_specs=[pl.BlockSpec((tm, d), lambda m: (jnp.minimum(m, pt - 1), 0)),
                  pl.BlockSpec((tm, d), lambda m: (jnp.maximum(m - pt, 0), 0)),
                  pl.BlockSpec((1, d), lambda m: (0, 0))],
        out_specs=[row_out, row_out],
        out_shape=[jax.ShapeDtypeStruct((n_p + n_s, d), f32),
                   jax.ShapeDtypeStruct((n_p + n_s, d), bf16)],
        compiler_params=_params(("parallel",)),
        name="join_norm",
    )(x_p, x_s, g.reshape(1, d))


def _mm_kernel(*refs, scale, has_res):
    if has_res:
        h_ref, w_ref, r_ref, o_ref, wb_ref = refs
    else:
        h_ref, w_ref, o_ref, wb_ref = refs

    @pl.when(pl.program_id(1) == 0)
    def _():
        wb_ref[...] = w_ref[...].astype(bf16)

    acc = jnp.dot(h_ref[...], wb_ref[...], preferred_element_type=f32)
    if scale != 1.0:
        acc = acc * scale
    if has_res:
        acc = acc + r_ref[...]
    o_ref[...] = acc.astype(o_ref.dtype)


def matmul(h, w, layer, col0, ncols, out_dtype, *, scale=1.0, res=None, row0=0, nrows=None):
    n, kdim = h.shape
    nrows = n - row0 if nrows is None else nrows
    tm = _pick(nrows, (768, 512, 256, 128, 64, 32))
    tn = _pick(ncols, (1024, 512, 256, 128))
    assert row0 % tm == 0 and col0 % tn == 0
    roff, coff = row0 // tm, col0 // tn
    in_specs = [pl.BlockSpec((tm, kdim), lambda j, m: (m + roff, 0)),
                pl.BlockSpec((None, kdim, tn), lambda j, m: (layer, 0, j + coff))]
    args = [h, w]
    if res is not None:
        assert row0 == 0 and col0 == 0
        in_specs.append(pl.BlockSpec((tm, tn), lambda j, m: (m, j)))
        args.append(res)
    return pl.pallas_call(
        functools.partial(_mm_kernel, scale=scale, has_res=res is not None),
        grid=(ncols // tn, nrows // tm),
        in_specs=in_specs,
        out_specs=pl.BlockSpec((tm, tn), lambda j, m: (m, j)),
        out_shape=jax.ShapeDtypeStruct((nrows, ncols), out_dtype),
        scratch_shapes=[pltpu.VMEM((kdim, tn), bf16)],
        compiler_params=_params(("arbitrary", "arbitrary")),
        name="matmul",
    )(*args)


def _kv_kernel(h_ref, w_ref, o5_ref, ob_ref, wb_ref, *, layer, fill):
    @pl.when(jnp.logical_and(pl.program_id(1) == 0, pl.program_id(2) == 0))
    def _():
        wb_ref[...] = w_ref[...].astype(bf16)

    def project():
        pw = 2 * HEAD_DIM
        n_pair = o5_ref.shape[1] // 2
        accs = [jnp.dot(h_ref[...], wb_ref[:, c * pw:(c + 1) * pw], preferred_element_type=f32)
                for c in range(n_pair)]
        for c in range(n_pair):
            ob_ref[:, c * pw:(c + 1) * pw] = accs[c].astype(bf16)
            o5_ref[:, 2 * c, :] = accs[c][:, :HEAD_DIM]
            o5_ref[:, 2 * c + 1, :] = accs[c][:, HEAD_DIM:]

    if fill:
        pl.when(pl.program_id(1) == layer)(project)

        @pl.when(pl.program_id(1) != layer)
        def _():
            o5_ref[...] = jnp.zeros_like(o5_ref)
    else:
        project()


def kv_matmul(h, w, layer, col0, nrows, depth, out_buf):
    n, kdim = h.shape
    ncols = N_HEADS * HEAD_DIM
    tm = _pick(nrows, (512, 256, 128, 64, 32))
    hpt = N_HEADS
    tn = hpt * HEAD_DIM
    coff = col0 // tn
    fill = out_buf is None
    last = nrows // tm - 1
    if fill:
        slab = lambda s: s
        row = lambda s, m: jnp.where(s == layer, m, last)
    else:
        slab = lambda s: layer
        row = lambda s, m: m
    in_specs = [pl.BlockSpec((tm, kdim), lambda j, s, m: (row(s, m), 0)),
                pl.BlockSpec((None, kdim, tn), lambda j, s, m: (layer, 0, j + coff),
                             pipeline_mode=pl.Buffered(1))]
    args = [h, w]
    kern = functools.partial(_kv_kernel, layer=layer, fill=fill)
    aliases = {}
    if not fill:
        in_specs.append(pl.BlockSpec(memory_space=pl.ANY))
        args.append(out_buf)
        aliases = {2: 0}
        kern = functools.partial(_drop_ref, kern, 2)
    return pl.pallas_call(
        kern,
        grid=(ncols // tn, depth if fill else 1, nrows // tm),
        in_specs=in_specs,
        out_specs=[pl.BlockSpec((None, tm, hpt, HEAD_DIM), lambda j, s, m: (slab(s), m, j, 0)),
                   pl.BlockSpec((tm, tn), lambda j, s, m: (row(s, m), j))],
        out_shape=[jax.ShapeDtypeStruct((depth, nrows, N_HEADS, HEAD_DIM), f32),
                   jax.ShapeDtypeStruct((nrows, ncols), bf16)],
        scratch_shapes=[pltpu.VMEM((kdim, tn), bf16)],
        input_output_aliases=aliases,
        compiler_params=_params(("arbitrary", "arbitrary", "arbitrary")),
        name="kv_matmul",
    )(*args)


def _pair_blockdiag(x2):
    lane = lax.broadcasted_iota(jnp.int32, x2.shape, 1)
    first = lane < HEAD_DIM
    zero = jnp.zeros_like(x2)
    return jnp.concatenate([jnp.where(first, x2, zero), jnp.where(first, zero, x2)], axis=0)


def _pair_cols(t0, t1, tq):
    return jnp.concatenate([jnp.broadcast_to(t0, (tq, LANES)), jnp.broadcast_to(t1, (tq, LANES))], axis=1)


def _sb_pair_block(q2, k2, v2, tri2, carry2, masked):
    tk = k2.shape[0]
    z = lax.dot_general(q2, _pair_blockdiag(k2), (((1,), (1,)), ((), ())), preferred_element_type=f32)
    lk = _log_sigmoid(-z)
    if masked:
        row = lax.broadcasted_iota(jnp.int32, z.shape, 0)
        col = lax.broadcasted_iota(jnp.int32, z.shape, 1)
        mask = jnp.where(col >= tk, col - tk, col) < row
        lkm = jnp.where(mask, lk, 0.0)
    else:
        lkm = lk
    hi = lkm.astype(bf16)
    lo = (lkm - hi.astype(f32)).astype(bf16)
    suf = (jnp.dot(hi, tri2, preferred_element_type=f32)
           + jnp.dot(lo, tri2, preferred_element_type=f32))
    arg = z + lk + suf
    if carry2 is not None:
        arg = arg + carry2
    w = jnp.exp(arg)
    if masked:
        w = jnp.where(mask, w, 0.0)
    out = jnp.dot(w.astype(bf16), _pair_blockdiag(v2), preferred_element_type=f32)
    t0 = jnp.sum(lkm[:, :tk], axis=1, keepdims=True)
    t1 = jnp.sum(lkm[:, tk:], axis=1, keepdims=True)
    return out, t0, t1


def _attn_kernel(*refs, tq, n_qblk, pairs, n_past_blk, layer):
    refs = list(refs)
    q_ref, k_ref, v_ref = refs[:3]
    pos = 3
    if n_past_blk:
        pk_hbm, pv_hbm = refs[3:5]
        pos = 5
    tri_ref = refs[pos]
    pos += 1
    trid_ref = tri_ref
    if tq != LANES:
        trid_ref = refs[pos]
        pos += 1
    o_ref = refs[pos]
    pos += 1
    acc = refs[pos:pos + pairs]
    car = refs[pos + pairs:pos + 2 * pairs]
    pos += 2 * pairs
    if n_past_blk:
        kbuf, vbuf, sem = refs[pos:pos + 3]
    pw = 2 * HEAD_DIM
    heads = 2 * pairs

    def ps(p):
        return slice(p * pw, (p + 1) * pw)

    def sweep(q2s, j0, load_kv, before=None):
        tri2 = tri_ref[...]

        def cond(st):
            j, mx = st
            return jnp.logical_and(j >= 0, mx > SB_EXIT)

        def body(st):
            j, _ = st
            if before is not None:
                before(j)
            mx = None
            for p in range(pairs):
                k2, v2 = load_kv(j, p)
                c2 = car[p][...]
                out, t0, t1 = _sb_pair_block(q2s[p], k2, v2, tri2, c2, False)
                acc[p][...] = acc[p][...] + out
                c2 = c2 + _pair_cols(t0, t1, tq)
                car[p][...] = c2
                m = jnp.max(c2)
                mx = m if mx is None else jnp.maximum(mx, m)
            return j - 1, mx

        mx0 = None
        for p in range(pairs):
            m = jnp.max(car[p][...])
            mx0 = m if mx0 is None else jnp.maximum(mx0, m)
        return lax.while_loop(cond, body, (j0, mx0))

    if n_past_blk:
        b = pl.program_id(0)
        h0 = pl.multiple_of(pl.program_id(1) * heads, heads)

        def cache_copies(j, slot):
            rows = pl.ds(pl.multiple_of(j * LANES, LANES), LANES)
            return (pltpu.make_async_copy(pk_hbm.at[layer, b, rows, pl.ds(h0, heads), :], kbuf.at[slot],
                                          sem.at[0, slot]),
                    pltpu.make_async_copy(pv_hbm.at[layer, b, rows, pl.ds(h0, heads), :], vbuf.at[slot],
                                          sem.at[1, slot]))

        def cache_start(j, slot):
            for c in cache_copies(j, slot):
                c.start()

        def cache_wait(j, slot):
            for c in cache_copies(j, slot):
                c.wait()

    def qblock(qb, _):
        r0 = pl.multiple_of(qb * tq, tq)
        if n_past_blk:
            cache_start(n_past_blk - 1, (n_past_blk - 1) % 2)
        q2s = [q_ref[pl.ds(r0, tq), ps(p)] for p in range(pairs)]
        trid = trid_ref[...]
        for p in range(pairs):
            kd = k_ref[pl.ds(r0, tq), ps(p)].astype(bf16)
            vd = v_ref[pl.ds(r0, tq), ps(p)].astype(bf16)
            out, t0, t1 = _sb_pair_block(q2s[p], kd, vd, trid, None, True)
            acc[p][...] = out
            car[p][...] = _pair_cols(t0, t1, tq)

        if n_qblk > 1:
            def load_own(j, p):
                rj = pl.multiple_of(j * tq, tq)
                return (k_ref[pl.ds(rj, tq), ps(p)].astype(bf16),
                        v_ref[pl.ds(rj, tq), ps(p)].astype(bf16))
            sweep(q2s, qb - 1, load_own)

        if n_past_blk:
            def before(j):
                cache_wait(j, j % 2)

                @pl.when(j >= 1)
                def _():
                    cache_start(j - 1, (j - 1) % 2)

            def load_past(j, p):
                kb = kbuf.at[j % 2]
                vb = vbuf.at[j % 2]
                k2 = jnp.concatenate([kb[:, 2 * p, :], kb[:, 2 * p + 1, :]], axis=1)
                v2 = jnp.concatenate([vb[:, 2 * p, :], vb[:, 2 * p + 1, :]], axis=1)
                return k2.astype(bf16), v2.astype(bf16)

            j_end, _ = sweep(q2s, n_past_blk - 1, load_past, before)

            @pl.when(j_end >= 0)
            def _():
                cache_wait(j_end, j_end % 2)

        for p in range(pairs):
            o_ref[pl.ds(r0, tq), ps(p)] = acc[p][...].astype(o_ref.dtype)
        return 0

    if n_qblk == 1:
        qblock(0, 0)
    else:
        lax.fori_loop(0, n_qblk, qblock, 0)


def _attn_wave_kernel(q_ref, k_ref, v_ref, tri_ref, o_ref, *scratch, n_qblk, pairs, win):
    acc = scratch[:pairs]
    car = scratch[pairs:2 * pairs]
    tq = LANES
    pw = 2 * HEAD_DIM

    def ps(p):
        return slice(p * pw, (p + 1) * pw)

    def step(j, nact):
        r0 = _aligned(j * tq, tq)
        tri2 = tri_ref[...]
        row = lax.broadcasted_iota(jnp.int32, (tq, 2 * tq), 0)
        col = lax.broadcasted_iota(jnp.int32, (tq, 2 * tq), 1)
        mask = jnp.where(col >= tq, col - tq, col) < row
        zs = []
        for p in range(pairs):
            q_all = q_ref[pl.ds(r0, nact * tq), ps(p)]
            kk = _pair_blockdiag(k_ref[pl.ds(r0, tq), ps(p)])
            zs.append(lax.dot_general(q_all, kk, (((1,), (1,)), ((), ())), preferred_element_type=f32))
        sufs, argss, totss = [], [], []
        for p in range(pairs):
            his, los, args, tots = [], [], [], []
            for i in range(nact):
                zi = zs[p][i * tq:(i + 1) * tq]
                lk = _log_sigmoid(-zi)
                lkm = jnp.where(mask, lk, 0.0) if i == 0 else lk
                hi = lkm.astype(bf16)
                his.append(hi)
                los.append((lkm - hi.astype(f32)).astype(bf16))
                args.append(zi + lk)
                tots.append(_pair_cols(jnp.sum(lkm[:, :tq], axis=1, keepdims=True),
                                       jnp.sum(lkm[:, tq:], axis=1, keepdims=True), tq))
            sufs.append(jnp.dot(jnp.concatenate(his, axis=0), tri2, preferred_element_type=f32)
                        + jnp.dot(jnp.concatenate(los, axis=0), tri2, preferred_element_type=f32))
            argss.append(args)
            totss.append(tots)
        outs = []
        for p in range(pairs):
            ws = []
            for i in range(nact):
                a = argss[p][i] + sufs[p][i * tq:(i + 1) * tq]
                if i == 0:
                    w = jnp.where(mask, jnp.exp(a), 0.0)
                else:
                    w = jnp.exp(a + car[p][(j + i) % win])
                ws.append(w.astype(bf16))
            vv = _pair_blockdiag(v_ref[pl.ds(r0, tq), ps(p)])
            outs.append(jnp.dot(jnp.concatenate(ws, axis=0), vv, preferred_element_type=f32))
        for p in range(pairs):
            for i in range(nact):
                slot = (j + i) % win
                if i == 0:
                    acc[p][slot] = outs[p][:tq]
                    car[p][slot] = totss[p][0]
                else:
                    acc[p][slot] = acc[p][slot] + outs[p][i * tq:(i + 1) * tq]
                    car[p][slot] = car[p][slot] + totss[p][i]

    def finalize(qb, j0):
        slot = qb % win
        rq = _aligned(qb * tq, tq)
        if not (isinstance(j0, int) and j0 < 0):
            tri2 = tri_ref[...]
            q2s = [q_ref[pl.ds(rq, tq), ps(p)] for p in range(pairs)]

            def cond(st):
                j, mx = st
                return jnp.logical_and(j >= 0, mx > SB_EXIT)

            def body(st):
                j, _ = st
                rj = pl.multiple_of(j * tq, tq)
                mx = None
                for p in range(pairs):
                    c2 = car[p][slot]
                    out, t0, t1 = _sb_pair_block(q2s[p], k_ref[pl.ds(rj, tq), ps(p)], v_ref[pl.ds(rj, tq), ps(p)],
                                                 tri2, c2, False)
                    acc[p][slot] = acc[p][slot] + out
                    c2 = c2 + _pair_cols(t0, t1, tq)
                    car[p][slot] = c2
                    m = jnp.max(c2)
                    mx = m if mx is None else jnp.maximum(mx, m)
                return j - 1, mx

            mx0 = None
            for p in range(pairs):
                m = jnp.max(car[p][slot])
                mx0 = m if mx0 is None else jnp.maximum(mx0, m)
            lax.while_loop(cond, body, (j0, mx0))
        for p in range(pairs):
            o_ref[pl.ds(rq, tq), ps(p)] = acc[p][slot].astype(o_ref.dtype)

    n_full = max(n_qblk - win + 1, 0)
    for j in range(n_qblk - 1, n_full - 1, -1):
        step(j, n_qblk - j)

    if n_full:
        def full_step(i, _):
            j = n_full - 1 - i
            step(j, win)
            finalize(j + win - 1, j - 1)
            return 0
        lax.fori_loop(0, n_full, full_step, 0)

    for qb in range(min(win - 1, n_qblk) - 1, -1, -1):
        finalize(qb, -1)


def _tri_pair(t):
    tri = (jnp.arange(t)[:, None] > jnp.arange(t)[None, :]).astype(bf16)
    z = jnp.zeros((t, t), bf16)
    return jnp.concatenate([jnp.concatenate([tri, z], axis=1), jnp.concatenate([z, tri], axis=1)], axis=0)


def sb_attention(q, k, v, *, batch, seq, row0, layer=0, past_k=None, past_v=None):
    n, width = q.shape
    tq = min(seq, LANES)
    n_qblk = seq // tq
    heads = ATTN_HEADS
    pairs = heads // 2
    gw = heads * HEAD_DIM
    assert seq % tq == 0 and row0 % seq == 0 and width % gw == 0
    boff = row0 // seq
    in_specs = [pl.BlockSpec((seq, gw), lambda b, h: (b + boff, h)),
                pl.BlockSpec((seq, gw), lambda b, h: (b, h)),
                pl.BlockSpec((seq, gw), lambda b, h: (b, h))]
    args = [q, k, v]
    scratch = ([pltpu.VMEM((tq, 2 * HEAD_DIM), f32) for _ in range(pairs)]
               + [pltpu.VMEM((tq, 2 * LANES), f32) for _ in range(pairs)])
    n_past_blk = 0
    if past_k is not None:
        past = past_k.shape[2]
        assert past % LANES == 0
        n_past_blk = past // LANES
        in_specs += [pl.BlockSpec(memory_space=pl.ANY), pl.BlockSpec(memory_space=pl.ANY)]
        args += [past_k, past_v]
        scratch += [pltpu.VMEM((2, LANES, heads, HEAD_DIM), f32),
                    pltpu.VMEM((2, LANES, heads, HEAD_DIM), f32),
                    pltpu.SemaphoreType.DMA((2, 2))]
    in_specs.append(pl.BlockSpec((2 * LANES, 2 * LANES), lambda b, h: (0, 0)))
    args.append(_tri_pair(LANES))
    if tq != LANES:
        in_specs.append(pl.BlockSpec((2 * tq, 2 * tq), lambda b, h: (0, 0)))
        args.append(_tri_pair(tq))
    if past_k is None and n_qblk > 1:
        win = ATTN_WINDOW
        scratch = ([pltpu.VMEM((win, tq, 2 * HEAD_DIM), f32) for _ in range(pairs)]
                   + [pltpu.VMEM((win, tq, 2 * LANES), f32) for _ in range(pairs)])
        kern = functools.partial(_attn_wave_kernel, n_qblk=n_qblk, pairs=pairs, win=win)
    else:
        kern = functools.partial(_attn_kernel, tq=tq, n_qblk=n_qblk, pairs=pairs, n_past_blk=n_past_blk,
                                 layer=layer)
    return pl.pallas_call(
        kern,
        grid=(batch, width // gw),
        in_specs=in_specs,
        out_specs=pl.BlockSpec((seq, gw), lambda b, h: (b, h)),
        out_shape=jax.ShapeDtypeStruct((batch * seq, width), bf16),
        scratch_shapes=scratch,
        compiler_params=_params(("parallel", "parallel")),
        name="sb_attention",
    )(*args)


def _one_minus_sq(log_a, a):
    y = 2.0 * log_a
    series = -y * (1.0 + y * (1.0 / 2 + y * (1.0 / 6)))
    return jnp.where(y > -0.01, series, 1.0 - a * a)


def _lru_kernel(xb_ref, yg_ref, cprev_ref, h0_ref, cw_ref, cb_ref, wa_ref, wx_ref, ba_ref, bx_ref,
                lam_ref, ob_ref, cnew_ref, hlast_ref, xpad_s, a_s, u_s, hs_s, h_s, *, tt, n_chunks):
    tc = pl.program_id(1)
    npad = SUBLANES
    nprev = CONV_W - 1

    @pl.when(tc == 0)
    def _():
        xpad_s[npad - nprev:npad, :] = cprev_ref[...]
        h_s[...] = h0_ref[...]

    @pl.when(tc > 0)
    def _():
        xpad_s[npad - nprev:npad, :] = xpad_s[npad + tt - nprev:npad + tt, :]

    xpad_s[npad:npad + tt, :] = xb_ref[...]
    xc = jnp.broadcast_to(cb_ref[...], (tt, cb_ref.shape[1]))
    for tap in range(CONV_W):
        s = npad - nprev + tap
        xc = xc + xpad_s[s:s + tt, :] * cw_ref[tap:tap + 1, :]

    log_sig_lam = _log_sigmoid(lam_ref[...])
    bw = LANES
    for nb in range(LRU_BLOCKS):
        cs = slice(nb * bw, (nb + 1) * bw)
        xcb = xc[:, cs]
        xcb16 = xcb.astype(bf16)
        r = jax.nn.sigmoid(jnp.dot(xcb16, wa_ref[nb], preferred_element_type=f32) + ba_ref[:, cs])
        i = jax.nn.sigmoid(jnp.dot(xcb16, wx_ref[nb], preferred_element_type=f32) + bx_ref[:, cs])
        log_a = (LRU_C * r) * log_sig_lam[:, cs]
        a = jnp.exp(log_a)
        a_s[:, cs] = a
        u_s[:, cs] = jnp.sqrt(_one_minus_sq(log_a, a)) * (i * xcb)

    def step(t, h):
        h = a_s[pl.ds(t, 1), :] * h + u_s[pl.ds(t, 1), :]
        hs_s[pl.ds(t, 1), :] = h
        return h

    h = lax.fori_loop(0, tt, step, h_s[...], unroll=8)
    h_s[...] = h
    ob_ref[...] = (jax.nn.gelu(yg_ref[...].astype(f32)) * hs_s[...]).astype(ob_ref.dtype)

    @pl.when(tc == n_chunks - 1)
    def _():
        cnew_ref[...] = xpad_s[npad + tt - nprev:npad + tt, :]
        hlast_ref[...] = h


def rg_lru_mixer(xb, yg, conv_prev, h0, conv_w, conv_b, wa, wx, ba, bx, lam, *, batch, seq, row0):
    n, width = xb.shape
    tt = min(seq, 256)
    n_chunks = seq // tt
    assert seq % tt == 0 and row0 % tt == 0
    coff = row0 // tt
    row_spec = pl.BlockSpec((tt, width), lambda b, c: (coff + b * n_chunks + c, 0))
    out_row_spec = pl.BlockSpec((tt, width), lambda b, c: (b * n_chunks + c, 0))
    vec = lambda r: pl.BlockSpec((r, width), lambda b, c: (0, 0))
    blk = pl.BlockSpec((LRU_BLOCKS, LANES, LANES), lambda b, c: (0, 0, 0))
    in_specs = [row_spec, row_spec,
                pl.BlockSpec((None, CONV_W - 1, width), lambda b, c: (b, 0, 0)),
                pl.BlockSpec((None, 1, width), lambda b, c: (b, 0, 0)),
                vec(CONV_W), vec(1), blk, blk, vec(1), vec(1), vec(1)]
    args = [xb, yg, conv_prev, h0.reshape(batch, 1, width), conv_w, conv_b.reshape(1, width),
            wa.astype(bf16), wx.astype(bf16), ba.reshape(1, width), bx.reshape(1, width),
            lam.reshape(1, width)]
    ob, cnew, hlast = pl.pallas_call(
        functools.partial(_lru_kernel, tt=tt, n_chunks=n_chunks),
        grid=(batch, n_chunks),
        in_specs=in_specs,
        out_specs=[out_row_spec,
                   pl.BlockSpec((None, CONV_W - 1, width), lambda b, c: (b, 0, 0)),
                   pl.BlockSpec((None, 1, width), lambda b, c: (b, 0, 0))],
        out_shape=[jax.ShapeDtypeStruct((batch * seq, width), bf16),
                   jax.ShapeDtypeStruct((batch, CONV_W - 1, width), f32),
                   jax.ShapeDtypeStruct((batch, 1, width), f32)],
        scratch_shapes=[pltpu.VMEM((SUBLANES + tt, width), f32),
                        pltpu.VMEM((tt, width), f32),
                        pltpu.VMEM((tt, width), f32),
                        pltpu.VMEM((tt, width), f32),
                        pltpu.VMEM((1, width), f32)],
        compiler_params=_params(("parallel", "arbitrary")),
        name="rg_lru_mixer",
    )(*args)
    return ob, cnew, hlast.reshape(batch, width)


def _mix_kernel(oap_ref, oas_ref, obp_ref, obs_ref, ga_ref, gb_ref, wpa_ref, wpb_ref, o_ref, wa_s, wb_s,
                *, prompt_tiles):
    m = pl.program_id(1)

    @pl.when(m == 0)
    def _():
        wa_s[...] = wpa_ref[...].astype(bf16)
        wb_s[...] = wpb_ref[...].astype(bf16)

    def mix(oa_ref, ob_ref):
        pa = jnp.dot(oa_ref[...], wa_s[...], preferred_element_type=f32)
        pb = jnp.dot(ob_ref[...], wb_s[...], preferred_element_type=f32)
        mixed = (jax.nn.sigmoid(ga_ref[...].astype(f32)) * pa
                 + jax.nn.sigmoid(gb_ref[...].astype(f32)) * pb)
        o_ref[...] = mixed.astype(o_ref.dtype)

    pl.when(m < prompt_tiles)(functools.partial(mix, oap_ref, obp_ref))
    pl.when(m >= prompt_tiles)(functools.partial(mix, oas_ref, obs_ref))


def gated_mix(oa_p, oa_s, ob_p, ob_s, ga, gb, w_pa, w_pb, layer):
    n_p, kdim = oa_p.shape
    n_s = oa_s.shape[0]
    ncols = w_pa.shape[2]
    tm = _pick(n_s, (256, 128, 64, 32))
    assert n_p % tm == 0
    pt = n_p // tm
    tn = _pick(ncols, (1024, 512, 256, 128))
    act_p =pl.BlockSpec((tm, kdim), lambda j, m: (jnp.minimum(m, pt - 1), 0))
    act_s = pl.BlockSpec((tm, kdim), lambda j, m: (jnp.maximum(m - pt, 0), 0))
    gate = pl.BlockSpec((tm, tn), lambda j, m: (m, j))
    wsp = pl.BlockSpec((None, kdim, tn), lambda j, m: (layer, 0, j))
    return pl.pallas_call(
        functools.partial(_mix_kernel, prompt_tiles=pt),
        grid=(ncols // tn, (n_p + n_s) // tm),
        in_specs=[act_p, act_s, act_p, act_s, gate, gate, wsp, wsp],
        out_specs=gate,
        out_shape=jax.ShapeDtypeStruct((n_p + n_s, ncols), bf16),
        scratch_shapes=[pltpu.VMEM((kdim, tn), bf16), pltpu.VMEM((kdim, tn), bf16)],
        compiler_params=_params(("arbitrary", "arbitrary")),
        name="gated_mix",
    )(oa_p, oa_s, ob_p, ob_s, ga, gb, w_pa, w_pb)


def _route(logits, forced_group=None):
    lane = lax.broadcasted_iota(jnp.int32, logits.shape, 1).astype(f32)
    neg = jnp.float32(-3.0e38)
    big = jnp.float32(LANES)

    def first_argmax(vals):
        m = jnp.max(vals, axis=1, keepdims=True)
        idx = jnp.min(jnp.where(vals == m, lane, big), axis=1, keepdims=True)
        return m, idx

    group_lanes = lane < N_GROUPS
    gl = jnp.where(group_lanes, logits, neg)
    gmax, gidx = first_argmax(gl)
    if forced_group is None:
        g_logit = gmax
    else:
        gidx = jnp.full_like(gidx, forced_group)
        g_logit = jnp.sum(jnp.where(lane == gidx, logits, 0.0), axis=1, keepdims=True)
    p_g = jnp.exp(g_logit - gmax) / jnp.sum(jnp.where(group_lanes, jnp.exp(gl - gmax), 0.0), axis=1,
                                             keepdims=True)
    lo = ROUTE_EXPERT_LANE0 + EXPERTS_PER_GROUP * gidx
    sel = jnp.logical_and(lane >= lo, lane < lo + EXPERTS_PER_GROUP)
    sl = jnp.where(sel, logits, neg)
    m1, i1 = first_argmax(sl)
    sl2 = jnp.where(lane == i1, neg, sl)
    m2, i2 = first_argmax(sl2)
    e2 = jnp.exp(m2 - m1)
    den = 1.0 + e2
    w1 = (1.0 / den) * p_g
    w2 = (e2 / den) * p_g
    comb = jnp.where(lane == i1, w1, 0.0) + jnp.where(lane == i2, w2, 0.0)
    return gidx, comb


def _router_weights(wg, bg, ws, bs):
    d = wg.shape[0]
    pad = LANES - N_GROUPS - N_EXPERTS
    wsub = jnp.transpose(ws, (1, 0, 2)).reshape(d, N_EXPERTS)
    wr = jnp.concatenate([wg, wsub, jnp.zeros((d, pad), f32)], axis=1).astype(bf16)
    br = jnp.concatenate([bg, bs.reshape(-1), jnp.zeros((pad,), f32)]).reshape(1, LANES)
    return wr, br


def _router_kernel(x_ref, g_ref, wr_ref, br_ref, slab_ref, grp_ref):
    h = _rmsnorm_val(x_ref[...], g_ref[...]).astype(bf16)
    logits = jnp.dot(h, wr_ref[...], preferred_element_type=f32) + br_ref[...]
    gidx, _ = _route(logits)
    grp_ref[...] = jnp.broadcast_to(gidx, grp_ref.shape)
    _slab_store(slab_ref, h.astype(f32))


def router(x, g, wr, br):
    n, d = x.shape
    tm = _pick(n, (256, 128, 64, 32))
    s_per = d // LANES
    return pl.pallas_call(
        _router_kernel,
        grid=(n // tm,),
        in_specs=[pl.BlockSpec((tm, d), lambda m: (m, 0)),
                  pl.BlockSpec((1, d), lambda m: (0, 0)),
                  pl.BlockSpec((d, LANES), lambda m: (0, 0)),
                  pl.BlockSpec((1, LANES), lambda m: (0, 0))],
        out_specs=[pl.BlockSpec((tm * s_per, LANES), lambda m: (m, 0)),
                   pl.BlockSpec((tm, LANES), lambda m: (m, 0))],
        out_shape=[jax.ShapeDtypeStruct((n * s_per, LANES), f32),
                   jax.ShapeDtypeStruct((n, LANES), f32)],
        compiler_params=_params(("parallel",)),
        name="router",
    )(x, g.reshape(1, d), wr, br)


def _group_layout(grp, tm):
    n = grp.shape[0]
    g = grp[:, 0].astype(jnp.int32)
    onehot = (g[:, None] == jnp.arange(N_GROUPS, dtype=jnp.int32)[None, :]).astype(jnp.int32)
    csum = jnp.cumsum(onehot, axis=0)
    counts = csum[-1]
    rank = jnp.sum(onehot * csum, axis=1) - 1
    tiles = (counts + tm - 1) // tm
    tile_end = jnp.cumsum(tiles)
    tile_start = tile_end - tiles
    pos = (jnp.sum(onehot * tile_start[None, :], axis=1) * tm + rank).astype(jnp.int32)
    nt_max = (n + N_GROUPS * (tm - 1)) // tm
    src = jnp.zeros((nt_max * tm,), jnp.int32).at[pos].set(jnp.arange(n, dtype=jnp.int32))
    tix = jnp.arange(nt_max, dtype=jnp.int32)
    tile_group = jnp.minimum(jnp.sum((tix[:, None] >= tile_end[None, :]).astype(jnp.int32), axis=1),
                             N_GROUPS - 1).astype(jnp.int32)
    return src, pos, tile_group, tile_end[-1:].astype(jnp.int32)


def _moe_kernel(src_ref, tg_ref, nt_ref, slab_hbm, wr_ref, br_ref, wu_ref, wd_ref, ys_ref,
                gbuf, sem, lhs_s, comb_s, acc_s, *, tm, s_per, ff, rows):
    t = pl.program_id(0)
    e = pl.program_id(1)
    nt = nt_ref[0]
    last_e = EXPERTS_PER_GROUP - 1
    nxt = jnp.minimum(t + 1, nt - 1)

    def row_copy(row, i):
        return pltpu.make_async_copy(slab_hbm.at[pl.ds(pl.multiple_of(row * s_per, s_per), s_per), :],
                                     gbuf.at[pl.ds(pl.multiple_of(i * s_per, s_per), s_per), :],
                                     sem.at[0])

    def issue(tile):
        def body(i, _):
            row_copy(src_ref[tile * tm + i], i).start()
            return 0
        lax.fori_loop(0, tm, body, 0, unroll=8)

    def wait_tile():
        pltpu.make_async_copy(slab_hbm.at[pl.ds(0, tm * s_per), :], gbuf, sem.at[0]).wait()

    def prepare(tile):
        buf = tile % 2
        for s in range(s_per):
            lhs_s[buf, :, s * LANES:(s + 1) * LANES] = gbuf[pl.ds(s, tm, stride=s_per), :].astype(bf16)
        logits = jnp.dot(lhs_s[buf], wr_ref[...], preferred_element_type=f32) + br_ref[...]
        _, comb = _route(logits, tg_ref[tile].astype(f32))
        comb_s[buf] = comb

    def product(first):
        buf = t % 2
        lane_e = ROUTE_EXPERT_LANE0 + EXPERTS_PER_GROUP * tg_ref[t] + e
        for r in range(tm // rows):
            rs = slice(r * rows, (r + 1) * rows)
            comb = comb_s[buf, rs, :]
            lane = lax.broadcasted_iota(jnp.int32, comb.shape, 1)
            c = jnp.sum(jnp.where(lane == lane_e, comb, 0.0), axis=1, keepdims=True)
            up = jnp.dot(lhs_s[buf, rs, :], wu_ref[...], preferred_element_type=f32)
            act = (jax.nn.silu(up[:, :ff]) * up[:, ff:]).astype(bf16)
            y = c * jnp.dot(act, wd_ref[...], preferred_element_type=f32)
            acc_s[rs, :] = y if first else acc_s[rs, :] + y

    @pl.when(jnp.logical_and(e == 0, t < nt))
    def _():
        @pl.when(t == 0)
        def _():
            issue(0)
            wait_tile()
            prepare(0)

        issue(nxt)
        product(True)

    @pl.when(jnp.logical_and(jnp.logical_and(e > 0, e < last_e), t < nt))
    def _():
        product(False)

    @pl.when(jnp.logical_and(e == last_e, t < nt))
    def _():
        wait_tile()
        product(False)
        prepare(nxt)
        _slab_store(ys_ref, acc_s)

    @pl.when(jnp.logical_and(e == last_e, t >= nt))
    def _():
        ys_ref[...] = jnp.zeros_like(ys_ref)


def moe_sorted(slab, src, tile_group, nt, wr, br, wu, wd, layer, d):
    tm = MOE_TILE
    s_per = d // LANES
    nt_max = src.shape[0] // tm
    ff2 = wu.shape[3]
    ff = ff2 // 2
    rows = 256

    def widx(t, e, src_r, tg_r, nt_r):
        return tg_r[t] * EXPERTS_PER_GROUP + e

    grid_spec = pltpu.PrefetchScalarGridSpec(
        num_scalar_prefetch=3,
        grid=(nt_max, EXPERTS_PER_GROUP),
        in_specs=[pl.BlockSpec(memory_space=pl.ANY),
                  pl.BlockSpec((d, LANES), lambda t, e, *_: (0, 0)),
                  pl.BlockSpec((1, LANES), lambda t, e, *_: (0, 0)),
                  pl.BlockSpec((None, None, d, ff2), lambda t, e, *s: (layer, widx(t, e, *s), 0, 0)),
                  pl.BlockSpec((None, None, ff, d), lambda t, e, *s: (layer, widx(t, e, *s), 0, 0))],
        out_specs=pl.BlockSpec((tm * s_per, LANES), lambda t, e, *_: (t, 0)),
        scratch_shapes=[pltpu.VMEM((tm * s_per, LANES), f32),
                        pltpu.SemaphoreType.DMA((1,)),
                        pltpu.VMEM((2, tm, d), bf16),
                        pltpu.VMEM((2, tm, LANES), f32),
                        pltpu.VMEM((tm, d), f32)])
    return pl.pallas_call(
        functools.partial(_moe_kernel, tm=tm, s_per=s_per, ff=ff, rows=rows),
        grid_spec=grid_spec,
        out_shape=jax.ShapeDtypeStruct((nt_max * tm * s_per, LANES), f32),
        compiler_params=_params(("arbitrary", "arbitrary")),
        name="moe_sorted",
    )(src, tile_group, nt, slab, wr, br, wu, wd)


def _unperm_kernel(pos_ref, x1_ref, g_ref, ys_hbm, *rest, tm, s_per, final, tok0):
    if final:
        y_ref, gbuf, sem = rest
    else:
        x_ref, h_ref, gbuf, sem = rest
    m = pl.program_id(0)
    nm = pl.num_programs(0)

    def row_copy(row, i, slot):
        return pltpu.make_async_copy(ys_hbm.at[pl.ds(pl.multiple_of(row * s_per, s_per), s_per), :],
                                     gbuf.at[slot, pl.ds(pl.multiple_of(i * s_per, s_per), s_per), :],
                                     sem.at[slot])

    def issue(tile, slot):
        def body(i, _):
            row_copy(pos_ref[tok0 + tile * tm + i], i, slot).start()
            return 0
        lax.fori_loop(0, tm, body, 0, unroll=8)

    def wait_tile(slot):
        pltpu.make_async_copy(ys_hbm.at[pl.ds(0, tm * s_per), :], gbuf.at[slot], sem.at[slot]).wait()

    @pl.when(m == 0)
    def _():
        issue(0, 0)

    @pl.when(m + 1 < nm)
    def _():
        issue(m + 1, (m + 1) % 2)

    wait_tile(m % 2)
    x = x1_ref[...] + _slab_load(gbuf.at[m % 2], tm, s_per)
    if final:
        y_ref[...] = _rmsnorm_val(x, g_ref[...])
    else:
        x_ref[...] = x
        h_ref[...] = _rmsnorm_val(x, g_ref[...]).astype(h_ref.dtype)


def add_experts_norm(x1, ys, pos, g, *, final, row0=0, nrows=None):
    n, d = x1.shape
    nrows = n - row0 if nrows is None else nrows
    tm = _pick(nrows, (256, 128, 64, 32))
    assert row0 % tm == 0
    off = row0 // tm
    s_per = d // LANES
    row_in = pl.BlockSpec((tm, d), lambda m, *_: (m + off, 0))
    row_out = pl.BlockSpec((tm, d), lambda m, *_: (m, 0))
    if final:
        out_specs = row_out
        out_shape = jax.ShapeDtypeStruct((nrows, d), f32)
    else:
        out_specs = [row_out, row_out]
        out_shape = [jax.ShapeDtypeStruct((nrows, d), f32), jax.ShapeDtypeStruct((nrows, d), bf16)]
    grid_spec = pltpu.PrefetchScalarGridSpec(
        num_scalar_prefetch=1,
        grid=(nrows // tm,),
        in_specs=[row_in,
                  pl.BlockSpec((1, d), lambda m, *_: (0, 0)),
                  pl.BlockSpec(memory_space=pl.ANY)],
        out_specs=out_specs,
        scratch_shapes=[pltpu.VMEM((2, tm * s_per, LANES), f32),
                        pltpu.SemaphoreType.DMA((2,))])
    return pl.pallas_call(
        functools.partial(_unperm_kernel, tm=tm, s_per=s_per, final=final, tok0=row0),
        grid_spec=grid_spec,
        out_shape=out_shape,
        compiler_params=_params(("arbitrary",)),
        name="add_experts_norm",
    )(pos, x1, g.reshape(1, d), ys)


def kernel(x_prompt, x_sample, cache_k, cache_v, state_conv, state_lru, norm1_g, w_in, conv_w, conv_b,
           lru_wa, lru_ba, lru_wx, lru_bx, lru_lambda, w_pa, w_pb, w_o, norm2_g, router_group_w,
           router_group_b, router_sub_w, router_sub_b, expert_w_up, expert_w_down, final_norm_g):
    bp, tp, d = x_prompt.shape
    bs, ts, _ = x_sample.shape
    depth = w_in.shape[0]
    sbw = N_HEADS * HEAD_DIM
    lw = conv_w.shape[2]
    n_p, n_s = bp * tp, bs * ts

    x, h1 = join_norm(x_prompt.reshape(n_p, d), x_sample.reshape(n_s, d), norm1_g[0])
    zero_conv = jnp.zeros((bp, CONV_W - 1, lw), f32)
    zero_lru = jnp.zeros((bp, lw), f32)
    wu16 = expert_w_up.astype(bf16)
    wd16 = expert_w_down.astype(bf16)

    kp = vp = None
    ks_l, vs_l, cp_l, cs_l, hp_l, hs_l = [], [], [], [], [], []
    for l in range(depth):
        q = matmul(h1, w_in, l, 0, sbw, bf16, scale=SB_SCALE)
        kp, kb = kv_matmul(h1, w_in, l, sbw, n_p, depth, kp)
        vp, vb = kv_matmul(h1, w_in, l, 2 * sbw, n_p, depth, vp)
        ks = matmul(h1, w_in, l, sbw, sbw, f32, row0=n_p, nrows=n_s)
        vs = matmul(h1, w_in, l, 2 * sbw, sbw, f32, row0=n_p, nrows=n_s)
        c0 = 3 * sbw
        xb = matmul(h1, w_in, l, c0, lw, f32)
        yg = matmul(h1, w_in, l, c0 + lw, lw, bf16)
        ga = matmul(h1, w_in, l, c0 + 2 * lw, d, bf16)
        gb = matmul(h1, w_in, l, c0 + 2 * lw + d, d, bf16)

        oa_p = sb_attention(q, kb, vb, batch=bp, seq=tp, row0=0)
        oa_s = sb_attention(q, ks, vs, batch=bs, seq=ts, row0=n_p, layer=l, past_k=cache_k, past_v=cache_v)

        lru_args = (conv_w[l], conv_b[l], lru_wa[l], lru_wx[l], lru_ba[l], lru_bx[l], lru_lambda[l])
        ob_p, cp, hp = rg_lru_mixer(xb, yg, zero_conv, zero_lru, *lru_args, batch=bp, seq=tp, row0=0)
        ob_s, cs, hs = rg_lru_mixer(xb, yg, state_conv[l], state_lru[l], *lru_args, batch=bs, seq=ts,
                                    row0=n_p)

        mixed = gated_mix(oa_p, oa_s, ob_p, ob_s, ga, gb, w_pa, w_pb, l)
        x1 = matmul(mixed, w_o, l, 0, d, f32, res=x)

        wr, br = _router_weights(router_group_w[l], router_group_b[l], router_sub_w[l], router_sub_b[l])
        slab, grp = router(x1, norm2_g[l], wr, br)
        src, pos, tile_group, nt = _group_layout(grp, MOE_TILE)
        ys = moe_sorted(slab, src, tile_group, nt, wr, br, wu16, wd16, l, d)
        if l + 1 < depth:
            x, h1 = add_experts_norm(x1, ys, pos, norm1_g[l + 1], final=False)
        else:
            y_prompt = add_experts_norm(x1, ys, pos, final_norm_g, final=True, row0=0, nrows=n_p)
            y_sample = add_experts_norm(x1, ys, pos, final_norm_g, final=True, row0=n_p, nrows=n_s)

        ks_l.append(ks)
        vs_l.append(vs)
        cp_l.append(cp)
        cs_l.append(cs)
        hp_l.append(hp)
        hs_l.append(hs)

    prompt_k = kp.reshape(depth, bp, tp, N_HEADS, HEAD_DIM)
    prompt_v = vp.reshape(depth, bp, tp, N_HEADS, HEAD_DIM)
    sample_k = jnp.stack(ks_l).reshape(depth, bs, ts, N_HEADS, HEAD_DIM)
    sample_v = jnp.stack(vs_l).reshape(depth, bs, ts, N_HEADS, HEAD_DIM)
    return (y_prompt.reshape(bp, tp, d), y_sample.reshape(bs, ts, d), prompt_k, prompt_v,
            jnp.stack(cp_l), jnp.stack(hp_l), sample_k, sample_v, jnp.stack(cs_l), jnp.stack(hs_l))
```

```python
import functools

import jax
import jax.numpy as jnp
from jax import lax
from jax.experimental import pallas as pl
from jax.experimental.pallas import tpu as pltpu

N_HEADS = 16
HEAD_DIM = 128
SB_SCALE = HEAD_DIM ** -0.5
LRU_BLOCKS = 16
CONV_W = 4
LRU_C = 8.0
N_GROUPS = 4
EXPERTS_PER_GROUP = 4
N_EXPERTS = N_GROUPS * EXPERTS_PER_GROUP
NORM_EPS = 1e-6

LANES = 128
SUBLANES = 8
VMEM_LIMIT = 56 * 1024 * 1024
SB_EXIT = -110.0
ATTN_HEADS = 8
ATTN_WINDOW = 3
ROUTE_EXPERT_LANE0 = 4
MOE_TILE = 768

f32 = jnp.float32
bf16 = jnp.bfloat16


def _pick(n, cands):
    for c in cands:
        if n % c == 0:
            return c
    raise ValueError(f"no tile size for {n}")


def _params(sem):
    return pltpu.CompilerParams(dimension_semantics=sem, vmem_limit_bytes=VMEM_LIMIT)


def _drop_ref(kern, pos, *refs):
    kern(*(refs[:pos] + refs[pos + 1:]))


def _aligned(x, m):
    return x if isinstance(x, int) else pl.multiple_of(x, m)


def _log_sigmoid(x):
    return jnp.minimum(x, 0.0) - jnp.log(1.0 + jnp.exp(-jnp.abs(x)))


def _slab_store(ref, val):
    tm, d = val.shape
    s_per = d // LANES
    for s in range(s_per):
        ref[pl.ds(s, tm, stride=s_per), :] = val[:, s * LANES:(s + 1) * LANES]


def _slab_load(ref, tm, s_per):
    return jnp.concatenate([ref[pl.ds(s, tm, stride=s_per), :] for s in range(s_per)], axis=1)


def _rmsnorm_val(x, g):
    ms = jnp.mean(x * x, axis=-1, keepdims=True)
    return (x * lax.rsqrt(ms + NORM_EPS)) * g


def _join_norm_kernel(xp_ref, xs_ref, g_ref, x_ref, h_ref, *, prompt_tiles):
    def emit(src_ref):
        x = src_ref[...]
        x_ref[...] = x
        h_ref[...] = _rmsnorm_val(x, g_ref[...]).astype(h_ref.dtype)

    pl.when(pl.program_id(0) < prompt_tiles)(functools.partial(emit, xp_ref))
    pl.when(pl.program_id(0) >= prompt_tiles)(functools.partial(emit, xs_ref))


def join_norm(x_p, x_s, g):
    n_p, d = x_p.shape
    n_s = x_s.shape[0]
    tm = _pick(n_s, (256, 128, 64, 32, 16, 8))
    assert n_p % tm == 0
    pt = n_p // tm
    row_out = pl.BlockSpec((tm, d), lambda m: (m, 0))
    return pl.pallas_call(
        functools.partial(_join_norm_kernel, prompt_tiles=pt),
        grid=((n_p + n_s) // tm,),
        in_specs=[pl.BlockSpec((tm, d), lambda m: (jnp.minimum(m, pt - 1), 0)),
                  pl.BlockSpec((tm, d), lambda m: (jnp.maximum(m - pt, 0), 0)),
                  pl.BlockSpec((1, d), lambda m: (0, 0))],
        out_specs=[row_out, row_out],
        out_shape=[jax.ShapeDtypeStruct((n_p + n_s, d), f32),
                   jax.ShapeDtypeStruct((n_p + n_s, d), bf16)],
        compiler_params=_params(("parallel",)),
        name="join_norm",
    )(x_p, x_s, g.reshape(1, d))


def _mm_kernel(*refs, scale, has_res):
    if has_res:
        h_ref, w_ref, r_ref, o_ref, wb_ref = refs
    else:
        h_ref, w_ref, o_ref, wb_ref = refs

    @pl.when(pl.program_id(1) == 0)
    def _():
        wb_ref[...] = w_ref[...].astype(bf16)

    acc = jnp.dot(h_ref[...], wb_ref[...], preferred_element_type=f32)
    if scale != 1.0:
        acc = acc * scale
    if has_res:
        acc = acc + r_ref[...]
    o_ref[...] = acc.astype(o_ref.dtype)


def matmul(h, w, layer, col0, ncols, out_dtype, *, scale=1.0, res=None, row0=0, nrows=None):
    n, kdim = h.shape
    nrows = n - row0 if nrows is None else nrows
    tm = _pick(nrows, (768, 512, 256, 128, 64, 32))
    tn = _pick(ncols, (1024, 512, 256, 128))
    assert row0 % tm == 0 and col0 % tn == 0
    roff, coff = row0 // tm, col0 // tn
    in_specs = [pl.BlockSpec((tm, kdim), lambda j, m: (m + roff, 0)),
                pl.BlockSpec((None, kdim, tn), lambda j, m: (layer, 0, j + coff))]
    args = [h, w]
    if res is not None:
        assert row0 == 0 and col0 == 0
        in_specs.append(pl.BlockSpec((tm, tn), lambda j, m: (m, j)))
        args.append(res)
    return pl.pallas_call(
        functools.partial(_mm_kernel, scale=scale, has_res=res is not None),
        grid=(ncols // tn, nrows // tm),
        in_specs=in_specs,
        out_specs=pl.BlockSpec((tm, tn), lambda j, m: (m, j)),
        out_shape=jax.ShapeDtypeStruct((nrows, ncols), out_dtype),
        scratch_shapes=[pltpu.VMEM((kdim, tn), bf16)],
        compiler_params=_params(("arbitrary", "arbitrary")),
        name="matmul",
    )(*args)


def _kv_kernel(h_ref, w_ref, o5_ref, ob_ref, wb_ref, *, layer, fill):
    @pl.when(jnp.logical_and(pl.program_id(1) == 0, pl.program_id(2) == 0))
    def _():
        wb_ref[...] = w_ref[...].astype(bf16)

    def project():
        pw = 2 * HEAD_DIM
        n_pair = o5_ref.shape[1] // 2
        accs = [jnp.dot(h_ref[...], wb_ref[:, c * pw:(c + 1) * pw], preferred_element_type=f32)
                for c in range(n_pair)]
        for c in range(n_pair):
            ob_ref[:, c * pw:(c + 1) * pw] = accs[c].astype(bf16)
            o5_ref[:, 2 * c, :] = accs[c][:, :HEAD_DIM]
            o5_ref[:, 2 * c + 1, :] = accs[c][:, HEAD_DIM:]

    if fill:
        pl.when(pl.program_id(1) == layer)(project)

        @pl.when(pl.program_id(1) != layer)
        def _():
            o5_ref[...] = jnp.zeros_like(o5_ref)
    else:
        project()


def kv_matmul(h, w, layer, col0, nrows, depth, out_buf):
    n, kdim = h.shape
    ncols = N_HEADS * HEAD_DIM
    tm = _pick(nrows, (512, 256, 128, 64, 32))
    hpt = N_HEADS
    tn = hpt * HEAD_DIM
    coff = col0 // tn
    fill = out_buf is None
    last = nrows // tm - 1
    if fill:
        slab = lambda s: s
        row = lambda s, m: jnp.where(s == layer, m, last)
    else:
        slab = lambda s: layer
        row = lambda s, m: m
    in_specs = [pl.BlockSpec((tm, kdim), lambda j, s, m: (row(s, m), 0)),
                pl.BlockSpec((None, kdim, tn), lambda j, s, m: (layer, 0, j + coff),
                             pipeline_mode=pl.Buffered(1))]
    args = [h, w]
    kern = functools.partial(_kv_kernel, layer=layer, fill=fill)
    aliases = {}
    if not fill:
        in_specs.append(pl.BlockSpec(memory_space=pl.ANY))
        args.append(out_buf)
        aliases = {2: 0}
        kern = functools.partial(_drop_ref, kern, 2)
    return pl.pallas_call(
        kern,
        grid=(ncols // tn, depth if fill else 1, nrows // tm),
        in_specs=in_specs,
        out_specs=[pl.BlockSpec((None, tm, hpt, HEAD_DIM), lambda j, s, m: (slab(s), m, j, 0)),
                   pl.BlockSpec((tm, tn), lambda j, s, m: (row(s, m), j))],
        out_shape=[jax.ShapeDtypeStruct((depth, nrows, N_HEADS, HEAD_DIM), f32),
                   jax.ShapeDtypeStruct((nrows, ncols), bf16)],
        scratch_shapes=[pltpu.VMEM((kdim, tn), bf16)],
        input_output_aliases=aliases,
        compiler_params=_params(("arbitrary", "arbitrary", "arbitrary")),
        name="kv_matmul",
    )(*args)


def _pair_blockdiag(x2):
    lane = lax.broadcasted_iota(jnp.int32, x2.shape, 1)
    first = lane < HEAD_DIM
    zero = jnp.zeros_like(x2)
    return jnp.concatenate([jnp.where(first, x2, zero), jnp.where(first, zero, x2)], axis=0)


def _pair_cols(t0, t1, tq):
    return jnp.concatenate([jnp.broadcast_to(t0, (tq, LANES)), jnp.broadcast_to(t1, (tq, LANES))], axis=1)


def _sb_pair_block(q2, k2, v2, tri2, carry2, masked):
    tk = k2.shape[0]
    z = lax.dot_general(q2, _pair_blockdiag(k2), (((1,), (1,)), ((), ())), preferred_element_type=f32)
    lk = _log_sigmoid(-z)
    if masked:
        row = lax.broadcasted_iota(jnp.int32, z.shape, 0)
        col = lax.broadcasted_iota(jnp.int32, z.shape, 1)
        mask = jnp.where(col >= tk, col - tk, col) < row
        lkm = jnp.where(mask, lk, 0.0)
    else:
        lkm = lk
    hi = lkm.astype(bf16)
    lo = (lkm - hi.astype(f32)).astype(bf16)
    suf = (jnp.dot(hi, tri2, preferred_element_type=f32)
           + jnp.dot(lo, tri2, preferred_element_type=f32))
    arg = z + lk + suf
    if carry2 is not None:
        arg = arg + carry2
    w = jnp.exp(arg)
    if masked:
        w = jnp.where(mask, w, 0.0)
    out = jnp.dot(w.astype(bf16), _pair_blockdiag(v2), preferred_element_type=f32)
    t0 = jnp.sum(lkm[:, :tk], axis=1, keepdims=True)
    t1 = jnp.sum(lkm[:, tk:], axis=1, keepdims=True)
    return out, t0, t1


def _attn_kernel(*refs, tq, n_qblk, pairs, n_past_blk, layer):
    refs = list(refs)
    q_ref, k_ref, v_ref = refs[:3]
    pos = 3
    if n_past_blk:
        pk_hbm, pv_hbm = refs[3:5]
        pos = 5
    tri_ref = refs[pos]
    pos += 1
    trid_ref = tri_ref
    if tq != LANES:
        trid_ref = refs[pos]
        pos += 1
    o_ref = refs[pos]
    pos += 1
    acc = refs[pos:pos + pairs]
    car = refs[pos + pairs:pos + 2 * pairs]
    pos += 2 * pairs
    if n_past_blk:
        kbuf, vbuf, sem = refs[pos:pos + 3]
    pw = 2 * HEAD_DIM
    heads = 2 * pairs

    def ps(p):
        return slice(p * pw, (p + 1) * pw)

    def sweep(q2s, j0, load_kv, before=None):
        tri2 = tri_ref[...]

        def cond(st):
            j, mx = st
            return jnp.logical_and(j >= 0, mx > SB_EXIT)

        def body(st):
            j, _ = st
            if before is not None:
                before(j)
            mx = None
            for p in range(pairs):
                k2, v2 = load_kv(j, p)
                c2 = car[p][...]
                out, t0, t1 = _sb_pair_block(q2s[p], k2, v2, tri2, c2, False)
                acc[p][...] = acc[p][...] + out
                c2 = c2 + _pair_cols(t0, t1, tq)
                car[p][...] = c2
                m = jnp.max(c2)
                mx = m if mx is None else jnp.maximum(mx, m)
            return j - 1, mx

        mx0 = None
        for p in range(pairs):
            m = jnp.max(car[p][...])
            mx0 = m if mx0 is None else jnp.maximum(mx0, m)
        return lax.while_loop(cond, body, (j0, mx0))

    if n_past_blk:
        b = pl.program_id(0)
        h0 = pl.multiple_of(pl.program_id(1) * heads, heads)

        def cache_copies(j, slot):
            rows = pl.ds(pl.multiple_of(j * LANES, LANES), LANES)
            return (pltpu.make_async_copy(pk_hbm.at[layer, b, rows, pl.ds(h0, heads), :], kbuf.at[slot],
                                          sem.at[0, slot]),
                    pltpu.make_async_copy(pv_hbm.at[layer, b, rows, pl.ds(h0, heads), :], vbuf.at[slot],
                                          sem.at[1, slot]))

        def cache_start(j, slot):
            for c in cache_copies(j, slot):
                c.start()

        def cache_wait(j, slot):
            for c in cache_copies(j, slot):
                c.wait()

    def qblock(qb, _):
        r0 = pl.multiple_of(qb * tq, tq)
        if n_past_blk:
            cache_start(n_past_blk - 1, (n_past_blk - 1) % 2)
        q2s = [q_ref[pl.ds(r0, tq), ps(p)] for p in range(pairs)]
        trid = trid_ref[...]
        for p in range(pairs):
            kd = k_ref[pl.ds(r0, tq), ps(p)].astype(bf16)
            vd = v_ref[pl.ds(r0, tq), ps(p)].astype(bf16)
            out, t0, t1 = _sb_pair_block(q2s[p], kd, vd, trid, None, True)
            acc[p][...] = out
            car[p][...] = _pair_cols(t0, t1, tq)

        if n_qblk > 1:
            def load_own(j, p):
                rj = pl.multiple_of(j * tq, tq)
                return (k_ref[pl.ds(rj, tq), ps(p)].astype(bf16),
                        v_ref[pl.ds(rj, tq), ps(p)].astype(bf16))
            sweep(q2s, qb - 1, load_own)

        if n_past_blk:
            def before(j):
                cache_wait(j, j % 2)

                @pl.when(j >= 1)
                def _():
                    cache_start(j - 1, (j - 1) % 2)

            def load_past(j, p):
                kb = kbuf.at[j % 2]
                vb = vbuf.at[j % 2]
                k2 = jnp.concatenate([kb[:, 2 * p, :], kb[:, 2 * p + 1, :]], axis=1)
                v2 = jnp.concatenate([vb[:, 2 * p, :], vb[:, 2 * p + 1, :]], axis=1)
                return k2.astype(bf16), v2.astype(bf16)

            j_end, _ = sweep(q2s, n_past_blk - 1, load_past, before)

            @pl.when(j_end >= 0)
            def _():
                cache_wait(j_end, j_end % 2)

        for p in range(pairs):
            o_ref[pl.ds(r0, tq), ps(p)] = acc[p][...].astype(o_ref.dtype)
        return 0

    if n_qblk == 1:
        qblock(0, 0)
    else:
        lax.fori_loop(0, n_qblk, qblock, 0)


def _attn_wave_kernel(q_ref, k_ref, v_ref, tri_ref, o_ref, *scratch, n_qblk, pairs, win):
    acc = scratch[:pairs]
    car = scratch[pairs:2 * pairs]
    tq = LANES
    pw = 2 * HEAD_DIM

    def ps(p):
        return slice(p * pw, (p + 1) * pw)

    def step(j, nact):
        r0 = _aligned(j * tq, tq)
        tri2 = tri_ref[...]
        row = lax.broadcasted_iota(jnp.int32, (tq, 2 * tq), 0)
        col = lax.broadcasted_iota(jnp.int32, (tq, 2 * tq), 1)
        mask = jnp.where(col >= tq, col - tq, col) < row
        zs = []
        for p in range(pairs):
            q_all = q_ref[pl.ds(r0, nact * tq), ps(p)]
            kk = _pair_blockdiag(k_ref[pl.ds(r0, tq), ps(p)])
            zs.append(lax.dot_general(q_all, kk, (((1,), (1,)), ((), ())), preferred_element_type=f32))
        sufs, argss, totss = [], [], []
        for p in range(pairs):
            his, los, args, tots = [], [], [], []
            for i in range(nact):
                zi = zs[p][i * tq:(i + 1) * tq]
                lk = _log_sigmoid(-zi)
                lkm = jnp.where(mask, lk, 0.0) if i == 0 else lk
                hi = lkm.astype(bf16)
                his.append(hi)
                los.append((lkm - hi.astype(f32)).astype(bf16))
                args.append(zi + lk)
                tots.append(_pair_cols(jnp.sum(lkm[:, :tq], axis=1, keepdims=True),
                                       jnp.sum(lkm[:, tq:], axis=1, keepdims=True), tq))
            sufs.append(jnp.dot(jnp.concatenate(his, axis=0), tri2, preferred_element_type=f32)
                        + jnp.dot(jnp.concatenate(los, axis=0), tri2, preferred_element_type=f32))
            argss.append(args)
            totss.append(tots)
        outs = []
        for p in range(pairs):
            ws = []
            for i in range(nact):
                a = argss[p][i] + sufs[p][i * tq:(i + 1) * tq]
                if i == 0:
                    w = jnp.where(mask, jnp.exp(a), 0.0)
                else:
                    w = jnp.exp(a + car[p][(j + i) % win])
                ws.append(w.astype(bf16))
            vv = _pair_blockdiag(v_ref[pl.ds(r0, tq), ps(p)])
            outs.append(jnp.dot(jnp.concatenate(ws, axis=0), vv, preferred_element_type=f32))
        for p in range(pairs):
            for i in range(nact):
                slot = (j + i) % win
                if i == 0:
                    acc[p][slot] = outs[p][:tq]
                    car[p][slot] = totss[p][0]
                else:
                    acc[p][slot] = acc[p][slot] + outs[p][i * tq:(i + 1) * tq]
                    car[p][slot] = car[p][slot] + totss[p][i]

    def finalize(qb, j0):
        slot = qb % win
        rq = _aligned(qb * tq, tq)
        if not (isinstance(j0, int) and j0 < 0):
            tri2 = tri_ref[...]
            q2s = [q_ref[pl.ds(rq, tq), ps(p)] for p in range(pairs)]

            def cond(st):
                j, mx = st
                return jnp.logical_and(j >= 0, mx > SB_EXIT)

            def body(st):
                j, _ = st
                rj = pl.multiple_of(j * tq, tq)
                mx = None
                for p in range(pairs):
                    c2 = car[p][slot]
                    out, t0, t1 = _sb_pair_block(q2s[p], k_ref[pl.ds(rj, tq), ps(p)], v_ref[pl.ds(rj, tq), ps(p)],
                                                 tri2, c2, False)
                    acc[p][slot] = acc[p][slot] + out
                    c2 = c2 + _pair_cols(t0, t1, tq)
                    car[p][slot] = c2
                    m = jnp.max(c2)
                    mx = m if mx is None else jnp.maximum(mx, m)
                return j - 1, mx

            mx0 = None
            for p in range(pairs):
                m = jnp.max(car[p][slot])
                mx0 = m if mx0 is None else jnp.maximum(mx0, m)
            lax.while_loop(cond, body, (j0, mx0))
        for p in range(pairs):
            o_ref[pl.ds(rq, tq), ps(p)] = acc[p][slot].astype(o_ref.dtype)

    n_full = max(n_qblk - win + 1, 0)
    for j in range(n_qblk - 1, n_full - 1, -1):
        step(j, n_qblk - j)

    if n_full:
        def full_step(i, _):
            j = n_full - 1 - i
            step(j, win)
            finalize(j + win - 1, j - 1)
            return 0
        lax.fori_loop(0, n_full, full_step, 0)

    for qb in range(min(win - 1, n_qblk) - 1, -1, -1):
        finalize(qb, -1)


def _tri_pair(t):
    tri = (jnp.arange(t)[:, None] > jnp.arange(t)[None, :]).astype(bf16)
    z = jnp.zeros((t, t), bf16)
    return jnp.concatenate([jnp.concatenate([tri, z], axis=1), jnp.concatenate([z, tri], axis=1)], axis=0)


def sb_attention(q, k, v, *, batch, seq, row0, layer=0, past_k=None, past_v=None):
    n, width = q.shape
    tq = min(seq, LANES)
    n_qblk = seq // tq
    heads = ATTN_HEADS
    pairs = heads // 2
    gw = heads * HEAD_DIM
    assert seq % tq == 0 and row0 % seq == 0 and width % gw == 0
    boff = row0 // seq
    in_specs = [pl.BlockSpec((seq, gw), lambda b, h: (b + boff, h)),
                pl.BlockSpec((seq, gw), lambda b, h: (b, h)),
                pl.BlockSpec((seq, gw), lambda b, h: (b, h))]
    args = [q, k, v]
    scratch = ([pltpu.VMEM((tq, 2 * HEAD_DIM), f32) for _ in range(pairs)]
               + [pltpu.VMEM((tq, 2 * LANES), f32) for _ in range(pairs)])
    n_past_blk = 0
    if past_k is not None:
        past = past_k.shape[2]
        assert past % LANES == 0
        n_past_blk = past // LANES
        in_specs += [pl.BlockSpec(memory_space=pl.ANY), pl.BlockSpec(memory_space=pl.ANY)]
        args += [past_k, past_v]
        scratch += [pltpu.VMEM((2, LANES, heads, HEAD_DIM), f32),
                    pltpu.VMEM((2, LANES, heads, HEAD_DIM), f32),
                    pltpu.SemaphoreType.DMA((2, 2))]
    in_specs.append(pl.BlockSpec((2 * LANES, 2 * LANES), lambda b, h: (0, 0)))
    args.append(_tri_pair(LANES))
    if tq != LANES:
        in_specs.append(pl.BlockSpec((2 * tq, 2 * tq), lambda b, h: (0, 0)))
        args.append(_tri_pair(tq))
    if past_k is None and n_qblk > 1:
        win = ATTN_WINDOW
        scratch = ([pltpu.VMEM((win, tq, 2 * HEAD_DIM), f32) for _ in range(pairs)]
                   + [pltpu.VMEM((win, tq, 2 * LANES), f32) for _ in range(pairs)])
        kern = functools.partial(_attn_wave_kernel, n_qblk=n_qblk, pairs=pairs, win=win)
    else:
        kern = functools.partial(_attn_kernel, tq=tq, n_qblk=n_qblk, pairs=pairs, n_past_blk=n_past_blk,
                                 layer=layer)
    return pl.pallas_call(
        kern,
        grid=(batch, width // gw),
        in_specs=in_specs,
        out_specs=pl.BlockSpec((seq, gw), lambda b, h: (b, h)),
        out_shape=jax.ShapeDtypeStruct((batch * seq, width), bf16),
        scratch_shapes=scratch,
        compiler_params=_params(("parallel", "parallel")),
        name="sb_attention",
    )(*args)


def _one_minus_sq(log_a, a):
    y = 2.0 * log_a
    series = -y * (1.0 + y * (1.0 / 2 + y * (1.0 / 6)))
    return jnp.where(y > -0.01, series, 1.0 - a * a)


def _lru_kernel(xb_ref, yg_ref, cprev_ref, h0_ref, cw_ref, cb_ref, wa_ref, wx_ref, ba_ref, bx_ref,
                lam_ref, ob_ref, cnew_ref, hlast_ref, xpad_s, a_s, u_s, hs_s, h_s, *, tt, n_chunks):
    tc = pl.program_id(1)
    npad = SUBLANES
    nprev = CONV_W - 1

    @pl.when(tc == 0)
    def _():
        xpad_s[npad - nprev:npad, :] = cprev_ref[...]
        h_s[...] = h0_ref[...]

    @pl.when(tc > 0)
    def _():
        xpad_s[npad - nprev:npad, :] = xpad_s[npad + tt - nprev:npad + tt, :]

    xpad_s[npad:npad + tt, :] = xb_ref[...]
    xc = jnp.broadcast_to(cb_ref[...], (tt, cb_ref.shape[1]))
    for tap in range(CONV_W):
        s = npad - nprev + tap
        xc = xc + xpad_s[s:s + tt, :] * cw_ref[tap:tap + 1, :]

    log_sig_lam = _log_sigmoid(lam_ref[...])
    bw = LANES
    for nb in range(LRU_BLOCKS):
        cs = slice(nb * bw, (nb + 1) * bw)
        xcb = xc[:, cs]
        xcb16 = xcb.astype(bf16)
        r = jax.nn.sigmoid(jnp.dot(xcb16, wa_ref[nb], preferred_element_type=f32) + ba_ref[:, cs])
        i = jax.nn.sigmoid(jnp.dot(xcb16, wx_ref[nb], preferred_element_type=f32) + bx_ref[:, cs])
        log_a = (LRU_C * r) * log_sig_lam[:, cs]
        a = jnp.exp(log_a)
        a_s[:, cs] = a
        u_s[:, cs] = jnp.sqrt(_one_minus_sq(log_a, a)) * (i * xcb)

    def step(t, h):
        h = a_s[pl.ds(t, 1), :] * h + u_s[pl.ds(t, 1), :]
        hs_s[pl.ds(t, 1), :] = h
        return h

    h = lax.fori_loop(0, tt, step, h_s[...], unroll=8)
    h_s[...] = h
    ob_ref[...] = (jax.nn.gelu(yg_ref[...].astype(f32)) * hs_s[...]).astype(ob_ref.dtype)

    @pl.when(tc == n_chunks - 1)
    def _():
        cnew_ref[...] = xpad_s[npad + tt - nprev:npad + tt, :]
        hlast_ref[...] = h


def rg_lru_mixer(xb, yg, conv_prev, h0, conv_w, conv_b, wa, wx, ba, bx, lam, *, batch, seq, row0):
    n, width = xb.shape
    tt = min(seq, 256)
    n_chunks = seq // tt
    assert seq % tt == 0 and row0 % tt == 0
    coff = row0 // tt
    row_spec = pl.BlockSpec((tt, width), lambda b, c: (coff + b * n_chunks + c, 0))
    out_row_spec = pl.BlockSpec((tt, width), lambda b, c: (b * n_chunks + c, 0))
    vec = lambda r: pl.BlockSpec((r, width), lambda b, c: (0, 0))
    blk = pl.BlockSpec((LRU_BLOCKS, LANES, LANES), lambda b, c: (0, 0, 0))
    in_specs = [row_spec, row_spec,
                pl.BlockSpec((None, CONV_W - 1, width), lambda b, c: (b, 0, 0)),
                pl.BlockSpec((None, 1, width), lambda b, c: (b, 0, 0)),
                vec(CONV_W), vec(1), blk, blk, vec(1), vec(1), vec(1)]
    args = [xb, yg, conv_prev, h0.reshape(batch, 1, width), conv_w, conv_b.reshape(1, width),
            wa.astype(bf16), wx.astype(bf16), ba.reshape(1, width), bx.reshape(1, width),
            lam.reshape(1, width)]
    ob, cnew, hlast = pl.pallas_call(
        functools.partial(_lru_kernel, tt=tt, n_chunks=n_chunks),
        grid=(batch, n_chunks),
        in_specs=in_specs,
        out_specs=[out_row_spec,
                   pl.BlockSpec((None, CONV_W - 1, width), lambda b, c: (b, 0, 0)),
                   pl.BlockSpec((None, 1, width), lambda b, c: (b, 0, 0))],
        out_shape=[jax.ShapeDtypeStruct((batch * seq, width), bf16),
                   jax.ShapeDtypeStruct((batch, CONV_W - 1, width), f32),
                   jax.ShapeDtypeStruct((batch, 1, width), f32)],
        scratch_shapes=[pltpu.VMEM((SUBLANES + tt, width), f32),
                        pltpu.VMEM((tt, width), f32),
                        pltpu.VMEM((tt, width), f32),
                        pltpu.VMEM((tt, width), f32),
                        pltpu.VMEM((1, width), f32)],
        compiler_params=_params(("parallel", "arbitrary")),
        name="rg_lru_mixer",
    )(*args)
    return ob, cnew, hlast.reshape(batch, width)


def _mix_kernel(oap_ref, oas_ref, obp_ref, obs_ref, ga_ref, gb_ref, wpa_ref, wpb_ref, o_ref, wa_s, wb_s,
                *, prompt_tiles):
    m = pl.program_id(1)

    @pl.when(m == 0)
    def _():
        wa_s[...] = wpa_ref[...].astype(bf16)
        wb_s[...] = wpb_ref[...].astype(bf16)

    def mix(oa_ref, ob_ref):
        pa = jnp.dot(oa_ref[...], wa_s[...], preferred_element_type=f32)
        pb = jnp.dot(ob_ref[...], wb_s[...], preferred_element_type=f32)
        mixed = (jax.nn.sigmoid(ga_ref[...].astype(f32)) * pa
                 + jax.nn.sigmoid(gb_ref[...].astype(f32)) * pb)
        o_ref[...] = mixed.astype(o_ref.dtype)

    pl.when(m < prompt_tiles)(functools.partial(mix, oap_ref, obp_ref))
    pl.when(m >= prompt_tiles)(functools.partial(mix, oas_ref, obs_ref))


def gated_mix(oa_p, oa_s, ob_p, ob_s, ga, gb, w_pa, w_pb, layer):
    n_p, kdim = oa_p.shape
    n_s = oa_s.shape[0]
    ncols = w_pa.shape[2]
    tm = _pick(n_s, (256, 128, 64, 32))
    assert n_p % tm == 0
    pt = n_p // tm
    tn = _pick(ncols, (1024, 512, 256, 128))
    act_p =pl.BlockSpec((tm, kdim), lambda j, m: (jnp.minimum(m, pt - 1), 0))
    act_s = pl.BlockSpec((tm, kdim), lambda j, m: (jnp.maximum(m - pt, 0), 0))
    gate = pl.BlockSpec((tm, tn), lambda j, m: (m, j))
    wsp = pl.BlockSpec((None, kdim, tn), lambda j, m: (layer, 0, j))
    return pl.pallas_call(
        functools.partial(_mix_kernel, prompt_tiles=pt),
        grid=(ncols // tn, (n_p + n_s) // tm),
        in_specs=[act_p, act_s, act_p, act_s, gate, gate, wsp, wsp],
        out_specs=gate,
        out_shape=jax.ShapeDtypeStruct((n_p + n_s, ncols), bf16),
        scratch_shapes=[pltpu.VMEM((kdim, tn), bf16), pltpu.VMEM((kdim, tn), bf16)],
        compiler_params=_params(("arbitrary", "arbitrary")),
        name="gated_mix",
    )(oa_p, oa_s, ob_p, ob_s, ga, gb, w_pa, w_pb)


def _route(logits, forced_group=None):
    lane = lax.broadcasted_iota(jnp.int32, logits.shape, 1).astype(f32)
    neg = jnp.float32(-3.0e38)
    big = jnp.float32(LANES)

    def first_argmax(vals):
        m = jnp.max(vals, axis=1, keepdims=True)
        idx = jnp.min(jnp.where(vals == m, lane, big), axis=1, keepdims=True)
        return m, idx

    group_lanes = lane < N_GROUPS
    gl = jnp.where(group_lanes, logits, neg)
    gmax, gidx = first_argmax(gl)
    if forced_group is None:
        g_logit = gmax
    else:
        gidx = jnp.full_like(gidx, forced_group)
        g_logit = jnp.sum(jnp.where(lane == gidx, logits, 0.0), axis=1, keepdims=True)
    p_g = jnp.exp(g_logit - gmax) / jnp.sum(jnp.where(group_lanes, jnp.exp(gl - gmax), 0.0), axis=1,
                                             keepdims=True)
    lo = ROUTE_EXPERT_LANE0 + EXPERTS_PER_GROUP * gidx
    sel = jnp.logical_and(lane >= lo, lane < lo + EXPERTS_PER_GROUP)
    sl = jnp.where(sel, logits, neg)
    m1, i1 = first_argmax(sl)
    sl2 = jnp.where(lane == i1, neg, sl)
    m2, i2 = first_argmax(sl2)
    e2 = jnp.exp(m2 - m1)
    den = 1.0 + e2
    w1 = (1.0 / den) * p_g
    w2 = (e2 / den) * p_g
    comb = jnp.where(lane == i1, w1, 0.0) + jnp.where(lane == i2, w2, 0.0)
    return gidx, comb


def _router_weights(wg, bg, ws, bs):
    d = wg.shape[0]
    pad = LANES - N_GROUPS - N_EXPERTS
    wsub = jnp.transpose(ws, (1, 0, 2)).reshape(d, N_EXPERTS)
    wr = jnp.concatenate([wg, wsub, jnp.zeros((d, pad), f32)], axis=1).astype(bf16)
    br = jnp.concatenate([bg, bs.reshape(-1), jnp.zeros((pad,), f32)]).reshape(1, LANES)
    return wr, br


def _router_kernel(x_ref, g_ref, wr_ref, br_ref, slab_ref, grp_ref):
    h = _rmsnorm_val(x_ref[...], g_ref[...]).astype(bf16)
    logits = jnp.dot(h, wr_ref[...], preferred_element_type=f32) + br_ref[...]
    gidx, _ = _route(logits)
    grp_ref[...] = jnp.broadcast_to(gidx, grp_ref.shape)
    _slab_store(slab_ref, h.astype(f32))


def router(x, g, wr, br):
    n, d = x.shape
    tm = _pick(n, (256, 128, 64, 32))
    s_per = d // LANES
    return pl.pallas_call(
        _router_kernel,
        grid=(n // tm,),
        in_specs=[pl.BlockSpec((tm, d), lambda m: (m, 0)),
                  pl.BlockSpec((1, d), lambda m: (0, 0)),
                  pl.BlockSpec((d, LANES), lambda m: (0, 0)),
                  pl.BlockSpec((1, LANES), lambda m: (0, 0))],
        out_specs=[pl.BlockSpec((tm * s_per, LANES), lambda m: (m, 0)),
                   pl.BlockSpec((tm, LANES), lambda m: (m, 0))],
        out_shape=[jax.ShapeDtypeStruct((n * s_per, LANES), f32),
                   jax.ShapeDtypeStruct((n, LANES), f32)],
        compiler_params=_params(("parallel",)),
        name="router",
    )(x, g.reshape(1, d), wr, br)


def _group_layout(grp, tm):
    n = grp.shape[0]
    g = grp[:, 0].astype(jnp.int32)
    onehot = (g[:, None] == jnp.arange(N_GROUPS, dtype=jnp.int32)[None, :]).astype(jnp.int32)
    csum = jnp.cumsum(onehot, axis=0)
    counts = csum[-1]
    rank = jnp.sum(onehot * csum, axis=1) - 1
    tiles = (counts + tm - 1) // tm
    tile_end = jnp.cumsum(tiles)
    tile_start = tile_end - tiles
    pos = (jnp.sum(onehot * tile_start[None, :], axis=1) * tm + rank).astype(jnp.int32)
    nt_max = (n + N_GROUPS * (tm - 1)) // tm
    src = jnp.zeros((nt_max * tm,), jnp.int32).at[pos].set(jnp.arange(n, dtype=jnp.int32))
    tix = jnp.arange(nt_max, dtype=jnp.int32)
    tile_group = jnp.minimum(jnp.sum((tix[:, None] >= tile_end[None, :]).astype(jnp.int32), axis=1),
                             N_GROUPS - 1).astype(jnp.int32)
    return src, pos, tile_group, tile_end[-1:].astype(jnp.int32)


def _moe_kernel(src_ref, tg_ref, nt_ref, slab_hbm, wr_ref, br_ref, wu_ref, wd_ref, ys_ref,
                gbuf, sem, lhs_s, comb_s, acc_s, *, tm, s_per, ff, rows):
    t = pl.program_id(0)
    e = pl.program_id(1)
    nt = nt_ref[0]
    last_e = EXPERTS_PER_GROUP - 1
    nxt = jnp.minimum(t + 1, nt - 1)

    def row_copy(row, i):
        return pltpu.make_async_copy(slab_hbm.at[pl.ds(pl.multiple_of(row * s_per, s_per), s_per), :],
                                     gbuf.at[pl.ds(pl.multiple_of(i * s_per, s_per), s_per), :],
                                     sem.at[0])

    def issue(tile):
        def body(i, _):
            row_copy(src_ref[tile * tm + i], i).start()
            return 0
        lax.fori_loop(0, tm, body, 0, unroll=8)

    def wait_tile():
        pltpu.make_async_copy(slab_hbm.at[pl.ds(0, tm * s_per), :], gbuf, sem.at[0]).wait()

    def prepare(tile):
        buf = tile % 2
        for s in range(s_per):
            lhs_s[buf, :, s * LANES:(s + 1) * LANES] = gbuf[pl.ds(s, tm, stride=s_per), :].astype(bf16)
        logits = jnp.dot(lhs_s[buf], wr_ref[...], preferred_element_type=f32) + br_ref[...]
        _, comb = _route(logits, tg_ref[tile].astype(f32))
        comb_s[buf] = comb

    def product(first):
        buf = t % 2
        lane_e = ROUTE_EXPERT_LANE0 + EXPERTS_PER_GROUP * tg_ref[t] + e
        for r in range(tm // rows):
            rs = slice(r * rows, (r + 1) * rows)
            comb = comb_s[buf, rs, :]
            lane = lax.broadcasted_iota(jnp.int32, comb.shape, 1)
            c = jnp.sum(jnp.where(lane == lane_e, comb, 0.0), axis=1, keepdims=True)
            up = jnp.dot(lhs_s[buf, rs, :], wu_ref[...], preferred_element_type=f32)
            act = (jax.nn.silu(up[:, :ff]) * up[:, ff:]).astype(bf16)
            y = c * jnp.dot(act, wd_ref[...], preferred_element_type=f32)
            acc_s[rs, :] = y if first else acc_s[rs, :] + y

    @pl.when(jnp.logical_and(e == 0, t < nt))
    def _():
        @pl.when(t == 0)
        def _():
            issue(0)
            wait_tile()
            prepare(0)

        issue(nxt)
        product(True)

    @pl.when(jnp.logical_and(jnp.logical_and(e > 0, e < last_e), t < nt))
    def _():
        product(False)

    @pl.when(jnp.logical_and(e == last_e, t < nt))
    def _():
        wait_tile()
        product(False)
        prepare(nxt)
        _slab_store(ys_ref, acc_s)

    @pl.when(jnp.logical_and(e == last_e, t >= nt))
    def _():
        ys_ref[...] = jnp.zeros_like(ys_ref)


def moe_sorted(slab, src, tile_group, nt, wr, br, wu, wd, layer, d):
    tm = MOE_TILE
    s_per = d // LANES
    nt_max = src.shape[0] // tm
    ff2 = wu.shape[3]
    ff = ff2 // 2
    rows = 256

    def widx(t, e, src_r, tg_r, nt_r):
        return tg_r[t] * EXPERTS_PER_GROUP + e

    grid_spec = pltpu.PrefetchScalarGridSpec(
        num_scalar_prefetch=3,
        grid=(nt_max, EXPERTS_PER_GROUP),
        in_specs=[pl.BlockSpec(memory_space=pl.ANY),
                  pl.BlockSpec((d, LANES), lambda t, e, *_: (0, 0)),
                  pl.BlockSpec((1, LANES), lambda t, e, *_: (0, 0)),
                  pl.BlockSpec((None, None, d, ff2), lambda t, e, *s: (layer, widx(t, e, *s), 0, 0)),
                  pl.BlockSpec((None, None, ff, d), lambda t, e, *s: (layer, widx(t, e, *s), 0, 0))],
        out_specs=pl.BlockSpec((tm * s_per, LANES), lambda t, e, *_: (t, 0)),
        scratch_shapes=[pltpu.VMEM((tm * s_per, LANES), f32),
                        pltpu.SemaphoreType.DMA((1,)),
                        pltpu.VMEM((2, tm, d), bf16),
                        pltpu.VMEM((2, tm, LANES), f32),
                        pltpu.VMEM((tm, d), f32)])
    return pl.pallas_call(
        functools.partial(_moe_kernel, tm=tm, s_per=s_per, ff=ff, rows=rows),
        grid_spec=grid_spec,
        out_shape=jax.ShapeDtypeStruct((nt_max * tm * s_per, LANES), f32),
        compiler_params=_params(("arbitrary", "arbitrary")),
        name="moe_sorted",
    )(src, tile_group, nt, slab, wr, br, wu, wd)


def _unperm_kernel(pos_ref, x1_ref, g_ref, ys_hbm, *rest, tm, s_per, final, tok0):
    if final:
        y_ref, gbuf, sem = rest
    else:
        x_ref, h_ref, gbuf, sem = rest
    m = pl.program_id(0)
    nm = pl.num_programs(0)

    def row_copy(row, i, slot):
        return pltpu.make_async_copy(ys_hbm.at[pl.ds(pl.multiple_of(row * s_per, s_per), s_per), :],
                                     gbuf.at[slot, pl.ds(pl.multiple_of(i * s_per, s_per), s_per), :],
                                     sem.at[slot])

    def issue(tile, slot):
        def body(i2, _):
            for prio in range(2):
                i = 2 * i2 + prio
                row_copy(pos_ref[tok0 + tile * tm + i], i, slot).start(priority=prio)
            return 0
        lax.fori_loop(0, tm // 2, body, 0, unroll=4)

    def wait_tile(slot):
        pltpu.make_async_copy(ys_hbm.at[pl.ds(0, tm * s_per), :], gbuf.at[slot], sem.at[slot]).wait()

    @pl.when(m == 0)
    def _():
        issue(0, 0)

    @pl.when(m + 1 < nm)
    def _():
        issue(m + 1, (m + 1) % 2)

    wait_tile(m % 2)
    x = x1_ref[...] + _slab_load(gbuf.at[m % 2], tm, s_per)
    if final:
        y_ref[...] = _rmsnorm_val(x, g_ref[...])
    else:
        x_ref[...] = x
        h_ref[...] = _rmsnorm_val(x, g_ref[...]).astype(h_ref.dtype)


def add_experts_norm(x1, ys, pos, g, *, final, row0=0, nrows=None):
    n, d = x1.shape
    nrows = n - row0 if nrows is None else nrows
    tm = _pick(nrows, (256, 128, 64, 32))
    assert row0 % tm == 0
    off = row0 // tm
    s_per = d // LANES
    row_in = pl.BlockSpec((tm, d), lambda m, *_: (m + off, 0))
    row_out = pl.BlockSpec((tm, d), lambda m, *_: (m, 0))
    if final:
        out_specs = row_out
        out_shape = jax.ShapeDtypeStruct((nrows, d), f32)
    else:
        out_specs = [row_out, row_out]
        out_shape = [jax.ShapeDtypeStruct((nrows, d), f32), jax.ShapeDtypeStruct((nrows, d), bf16)]
    grid_spec = pltpu.PrefetchScalarGridSpec(
        num_scalar_prefetch=1,
        grid=(nrows // tm,),
        in_specs=[row_in,
                  pl.BlockSpec((1, d), lambda m, *_: (0, 0)),
                  pl.BlockSpec(memory_space=pl.ANY)],
        out_specs=out_specs,
        scratch_shapes=[pltpu.VMEM((2, tm * s_per, LANES), f32),
                        pltpu.SemaphoreType.DMA((2,))])
    return pl.pallas_call(
        functools.partial(_unperm_kernel, tm=tm, s_per=s_per, final=final, tok0=row0),
        grid_spec=grid_spec,
        out_shape=out_shape,
        compiler_params=_params(("arbitrary",)),
        name="add_experts_norm",
    )(pos, x1, g.reshape(1, d), ys)


def kernel(x_prompt, x_sample, cache_k, cache_v, state_conv, state_lru, norm1_g, w_in, conv_w, conv_b,
           lru_wa, lru_ba, lru_wx, lru_bx, lru_lambda, w_pa, w_pb, w_o, norm2_g, router_group_w,
           router_group_b, router_sub_w, router_sub_b, expert_w_up, expert_w_down, final_norm_g):
    bp, tp, d = x_prompt.shape
    bs, ts, _ = x_sample.shape
    depth = w_in.shape[0]
    sbw = N_HEADS * HEAD_DIM
    lw = conv_w.shape[2]
    n_p, n_s = bp * tp, bs * ts

    x, h1 = join_norm(x_prompt.reshape(n_p, d), x_sample.reshape(n_s, d), norm1_g[0])
    zero_conv = jnp.zeros((bp, CONV_W - 1, lw), f32)
    zero_lru = jnp.zeros((bp, lw), f32)
    wu16 = expert_w_up.astype(bf16)
    wd16 = expert_w_down.astype(bf16)

    kp = vp = None
    ks_l, vs_l, cp_l, cs_l, hp_l, hs_l = [], [], [], [], [], []
    for l in range(depth):
        q = matmul(h1, w_in, l, 0, sbw, bf16, scale=SB_SCALE)
        kp, kb = kv_matmul(h1, w_in, l, sbw, n_p, depth, kp)
        vp, vb = kv_matmul(h1, w_in, l, 2 * sbw, n_p, depth, vp)
        ks = matmul(h1, w_in, l, sbw, sbw, f32, row0=n_p, nrows=n_s)
        vs = matmul(h1, w_in, l, 2 * sbw, sbw, f32, row0=n_p, nrows=n_s)
        c0 = 3 * sbw
        xb = matmul(h1, w_in, l, c0, lw, f32)
        yg = matmul(h1, w_in, l, c0 + lw, lw, bf16)
        ga = matmul(h1, w_in, l, c0 + 2 * lw, d, bf16)
        gb = matmul(h1, w_in, l, c0 + 2 * lw + d, d, bf16)

        oa_p = sb_attention(q, kb, vb, batch=bp, seq=tp, row0=0)
        oa_s = sb_attention(q, ks, vs, batch=bs, seq=ts, row0=n_p, layer=l, past_k=cache_k, past_v=cache_v)

        lru_args = (conv_w[l], conv_b[l], lru_wa[l], lru_wx[l], lru_ba[l], lru_bx[l], lru_lambda[l])
        ob_p, cp, hp = rg_lru_mixer(xb, yg, zero_conv, zero_lru, *lru_args, batch=bp, seq=tp, row0=0)
        ob_s, cs, hs = rg_lru_mixer(xb, yg, state_conv[l], state_lru[l], *lru_args, batch=bs, seq=ts,
                                    row0=n_p)

        mixed = gated_mix(oa_p, oa_s, ob_p, ob_s, ga, gb, w_pa, w_pb, l)
        x1 = matmul(mixed, w_o, l, 0, d, f32, res=x)

        wr, br = _router_weights(router_group_w[l], router_group_b[l], router_sub_w[l], router_sub_b[l])
        slab, grp = router(x1, norm2_g[l], wr, br)
        src, pos, tile_group, nt = _group_layout(grp, MOE_TILE)
        ys = moe_sorted(slab, src, tile_group, nt, wr, br, wu16, wd16, l, d)
        if l + 1 < depth:
            x, h1 = add_experts_norm(x1, ys, pos, norm1_g[l + 1], final=False)
        else:
            y_prompt = add_experts_norm(x1, ys, pos, final_norm_g, final=True, row0=0, nrows=n_p)
            y_sample = add_experts_norm(x1, ys, pos, final_norm_g, final=True, row0=n_p, nrows=n_s)

        ks_l.append(ks)
        vs_l.append(vs)
        cp_l.append(cp)
        cs_l.append(cs)
        hp_l.append(hp)
        hs_l.append(hs)

    prompt_k = kp.reshape(depth, bp, tp, N_HEADS, HEAD_DIM)
    prompt_v = vp.reshape(depth, bp, tp, N_HEADS, HEAD_DIM)
    sample_k = jnp.stack(ks_l).reshape(depth, bs, ts, N_HEADS, HEAD_DIM)
    sample_v = jnp.stack(vs_l).reshape(depth, bs, ts, N_HEADS, HEAD_DIM)
    return (y_prompt.reshape(bp, tp, d), y_sample.reshape(bs, ts, d), prompt_k, prompt_v,
            jnp.stack(cp_l), jnp.stack(hp_l), sample_k, sample_v, jnp.stack(cs_l), jnp.stack(hs_l))
```
